```python
import math
import jax, jax.numpy as jnp
from jax import lax
import numpy as np

D_MODEL = 2048
BATCH = 2
SEQ = 16384
DEPTH = 1

SB_HEADS = 8
SB_HEAD_DIM = D_MODEL // 16
SB_WIDTH = SB_HEADS * SB_HEAD_DIM
Q_BLOCK = 128
GM_GROUPS = 8
GM_GROUP_DIM = D_MODEL // 16
GM_WIDTH = GM_GROUPS * GM_GROUP_DIM
GM_CHUNK = 128
IN_WIDTH = 3 * SB_WIDTH + 2 * GM_WIDTH + 2 * D_MODEL
N_MEM = 256
X_HEADS = 4
X_HEAD_DIM = D_MODEL // 16
X_WIDTH = X_HEADS * X_HEAD_DIM
N_GROUPS = 4
EXPERTS_PER_GROUP = 8
N_EXPERTS = N_GROUPS * EXPERTS_PER_GROUP
TOP_K = 2
D_EXPERT = D_MODEL // 2
MOE_BLOCK = 128
EPS = 1e-6

kernel_name = "hybrid_stickbreak_gmlp_hmoe"


def rmsnorm(x, g):
    xf = x.astype(jnp.float32)
    y = xf * lax.rsqrt(jnp.mean(xf * xf, axis=-1, keepdims=True) + EPS)
    return (y * g.astype(jnp.float32)).astype(x.dtype)


def stick_breaking_attention(q, k, v):
    B, S, H, Dh = q.shape
    nb = S // Q_BLOCK
    scale = 1.0 / math.sqrt(Dh)
    kh = k.transpose(0, 2, 1, 3)
    vh = v.transpose(0, 2, 1, 3)
    qb = q.reshape(B, nb, Q_BLOCK, H, Dh).transpose(1, 0, 3, 2, 4)
    key_pos = jnp.arange(S)

    def one_block(args):
        q_i, i = args
        z = jnp.einsum('bhqd,bhkd->bhqk', q_i, kh, preferred_element_type=jnp.float32) * scale
        q_pos = i * Q_BLOCK + jnp.arange(Q_BLOCK)
        strict = key_pos[None, :] < q_pos[:, None]
        log_not = jnp.where(strict, jax.nn.log_sigmoid(-z), 0.0)
        between = lax.cumsum(log_not, axis=3, reverse=True) - log_not
        a = jnp.where(strict, jnp.exp(jax.nn.log_sigmoid(z) + between), 0.0)
        return jnp.einsum('bhqk,bhkd->bhqd', a.astype(vh.dtype), vh)

    o = lax.map(one_block, (qb, jnp.arange(nb)))
    return o.transpose(1, 0, 3, 2, 4).reshape(B, S, H * Dh)


def chunked_spatial_gating(u, v, g_norm, w_s, b_s):
    B, S, W = u.shape
    nc = S // GM_CHUNK
    v = rmsnorm(v, g_norm)
    vg = v.reshape(B, nc, GM_CHUNK, GM_GROUPS, GM_GROUP_DIM)
    mask = jnp.tril(jnp.ones((GM_CHUNK, GM_CHUNK), dtype=w_s.dtype))
    w = (w_s * mask[None]).astype(v.dtype)
    mixed = jnp.einsum('gts,bcsge->bctge', w, vg) + b_s.T.astype(v.dtype)[None, None, :, :, None]
    return u * mixed.reshape(B, S, W)


def memory_cross_attention(h, mem, w_xq, w_xkv, w_xo):
    B, S, _ = h.shape
    M = mem.shape[1]
    q = (h @ w_xq).reshape(B, S, X_HEADS, X_HEAD_DIM)
    k, v = jnp.split(mem @ w_xkv, 2, axis=-1)
    k = k.reshape(B, M, X_HEADS, X_HEAD_DIM)
    v = v.reshape(B, M, X_HEADS, X_HEAD_DIM)
    s = jnp.einsum('bqhd,bmhd->bhqm', q, k, preferred_element_type=jnp.float32) / math.sqrt(X_HEAD_DIM)
    p = jax.nn.softmax(s, axis=-1)
    o = jnp.einsum('bhqm,bmhd->bqhd', p.astype(v.dtype), v).reshape(B, S, X_WIDTH)
    return o @ w_xo


def hierarchical_moe(h, w_rg, b_rg, w_re, b_re, w_gate, w_up, w_down):
    B, S, D = h.shape
    T = B * S
    hf = h.reshape(T, D)
    g_logits = (hf @ w_rg).astype(jnp.float32) + b_rg.astype(jnp.float32)
    g_prob = jax.nn.softmax(g_logits, axis=-1)
    g_idx = jnp.argmax(g_logits, axis=-1).astype(jnp.int32)
    p_g = jnp.take_along_axis(g_prob, g_idx[:, None], axis=1)[:, 0]
    e_all = (hf @ w_re).astype(jnp.float32) + b_re.astype(jnp.float32)
    e_logits = jnp.take_along_axis(e_all.reshape(T, N_GROUPS, EXPERTS_PER_GROUP), g_idx[:, None, None], axis=1)[:, 0]
    top_v, top_i = lax.top_k(e_logits, TOP_K)
    weights = p_g[:, None] * jax.nn.softmax(top_v, axis=-1)
    expert = g_idx[:, None] * EXPERTS_PER_GROUP + top_i.astype(jnp.int32)
    n = T * TOP_K
    flat_e = expert.reshape(n)
    flat_tok = jnp.repeat(jnp.arange(T, dtype=jnp.int32), TOP_K)
    flat_w = weights.reshape(n)
    order = jnp.argsort(flat_e)
    sorted_e = flat_e[order]
    counts = jnp.bincount(flat_e, length=N_EXPERTS)
    padded = ((counts + MOE_BLOCK - 1) // MOE_BLOCK) * MOE_BLOCK
    starts = jnp.cumsum(counts) - counts
    pends = jnp.cumsum(padded)
    pstarts = pends - padded
    dest = pstarts[sorted_e] + jnp.arange(n) - starts[sorted_e]
    P = (-(-n // MOE_BLOCK) + N_EXPERTS) * MOE_BLOCK
    tok_buf = jnp.zeros((P,), jnp.int32).at[dest].set(flat_tok[order])
    w_buf = jnp.zeros((P,), jnp.float32).at[dest].set(flat_w[order])
    nblk = P // MOE_BLOCK
    blk_start = jnp.arange(nblk) * MOE_BLOCK
    blk_e = jnp.minimum(jnp.sum(pends[None, :] <= blk_start[:, None], axis=1), N_EXPERTS - 1)

    def run_block(args):
        tok, wt, e = args
        xb = hf[tok]
        y = (jax.nn.silu(xb @ w_gate[e]) * (xb @ w_up[e])) @ w_down[e]
        return y * wt[:, None].astype(y.dtype)

    ys = lax.map(run_block, (tok_buf.reshape(nblk, MOE_BLOCK), w_buf.reshape(nblk, MOE_BLOCK), blk_e))
    out = jnp.zeros((T, D), h.dtype).at[tok_buf].add(ys.reshape(P, D))
    return out.reshape(B, S, D)


def setup_inputs(seed: int = 0) -> dict:
    key = jax.random.key(seed)
    ks = jax.random.split(key, 24)
    f32 = jnp.float32
    L, D = DEPTH, D_MODEL

    def nrm(k, shape, fan_in):
        return jax.random.normal(k, shape, f32) * (fan_in ** -0.5)

    def gain(k, shape):
        return 1.0 + 0.02 * jax.random.normal(k, shape, f32)

    return {
        "x": jax.random.normal(ks[0], (BATCH, SEQ, D), f32),
        "mem": jax.random.normal(ks[1], (BATCH, N_MEM, D), f32),
        "g_mix": gain(ks[2], (L, D)),
        "w_in": nrm(ks[3], (L, D, IN_WIDTH), D),
        "g_gm": gain(ks[4], (L, GM_WIDTH)),
        "w_spatial": nrm(ks[5], (L, GM_GROUPS, GM_CHUNK, GM_CHUNK), GM_CHUNK),
        "b_spatial": 1.0 + 0.1 * jax.random.normal(ks[6], (L, GM_GROUPS, GM_CHUNK), f32),
        "w_branch_sb": nrm(ks[7], (L, SB_WIDTH, D), SB_WIDTH),
        "w_branch_gm": nrm(ks[8], (L, GM_WIDTH, D), GM_WIDTH),
        "w_out": nrm(ks[9], (L, D, D), D),
        "g_cross": gain(ks[10], (L, D)),
        "g_mem": gain(ks[11], (L, D)),
        "w_xq": nrm(ks[12], (L, D, X_WIDTH), D),
        "w_xkv": nrm(ks[13], (L, D, 2 * X_WIDTH), D),
        "w_xo": nrm(ks[14], (L, X_WIDTH, D), X_WIDTH),
        "g_ffn": gain(ks[15], (L, D)),
        "w_rg": nrm(ks[16], (L, D, N_GROUPS), D),
        "b_rg": 0.01 * jax.random.normal(ks[17], (L, N_GROUPS), f32),
        "w_re": nrm(ks[18], (L, D, N_EXPERTS), D),
        "b_re": 0.01 * jax.random.normal(ks[19], (L, N_EXPERTS), f32),
        "w_e_gate": nrm(ks[20], (L, N_EXPERTS, D, D_EXPERT), D),
        "w_e_up": nrm(ks[21], (L, N_EXPERTS, D, D_EXPERT), D),
        "w_e_down": nrm(ks[22], (L, N_EXPERTS, D_EXPERT, D), D_EXPERT),
        "g_final": gain(ks[23], (D,)),
    }


def reference(x, mem, g_mix, w_in, g_gm, w_spatial, b_spatial, w_branch_sb, w_branch_gm, w_out,
              g_cross, g_mem, w_xq, w_xkv, w_xo, g_ffn, w_rg, b_rg, w_re, b_re,
              w_e_gate, w_e_up, w_e_down, g_final):
    B, S, D = x.shape
    splits = [SB_WIDTH, 2 * SB_WIDTH, 3 * SB_WIDTH, 3 * SB_WIDTH + GM_WIDTH,
              3 * SB_WIDTH + 2 * GM_WIDTH, 3 * SB_WIDTH + 2 * GM_WIDTH + D_MODEL]
    for l in range(DEPTH):
        hn = rmsnorm(x, g_mix[l])
        proj = hn @ w_in[l]
        q, k, v, u_gm, v_gm, gate_sb, gate_gm = jnp.split(proj, splits, axis=-1)
        o_sb = stick_breaking_attention(q.reshape(B, S, SB_HEADS, SB_HEAD_DIM),
                                        k.reshape(B, S, SB_HEADS, SB_HEAD_DIM),
                                        v.reshape(B, S, SB_HEADS, SB_HEAD_DIM))
        o_gm = chunked_spatial_gating(jax.nn.gelu(u_gm), jax.nn.gelu(v_gm),
                                      g_gm[l], w_spatial[l], b_spatial[l])
        merged = (jax.nn.sigmoid(gate_sb) * (o_sb @ w_branch_sb[l])
                  + jax.nn.sigmoid(gate_gm) * (o_gm @ w_branch_gm[l]))
        x = x + merged @ w_out[l]
        x = x + memory_cross_attention(rmsnorm(x, g_cross[l]), rmsnorm(mem, g_mem[l]),
                                       w_xq[l], w_xkv[l], w_xo[l])
        x = x + hierarchical_moe(rmsnorm(x, g_ffn[l]), w_rg[l], b_rg[l], w_re[l], b_re[l],
                                 w_e_gate[l], w_e_up[l], w_e_down[l])
    return rmsnorm(x, g_final)
```

```python
import functools
import math

import jax
import jax.numpy as jnp
from jax import lax
from jax.experimental import pallas as pl
from jax.experimental.pallas import tpu as pltpu

F32 = jnp.float32
BF16 = jnp.bfloat16
I32 = jnp.int32

EPS = 1e-6
LANES = 128
F32_SUBLANES = 8
MIB = 1024 * 1024

SB_HEADS = 8
GM_GROUPS = 8
GM_CHUNK = 128
X_HEADS = 4
N_GROUPS = 4
EXPERTS_PER_GROUP = 8
N_EXPERTS = N_GROUPS * EXPERTS_PER_GROUP
TOP_K = 2

INPROJ_BM = 512
INPROJ_BN = 1024
ATTN_QB = 128
MERGE_BM = 256
CROSS_BM = 256
ROUTER_BM = 512
RANK_ROWS = 8
MOE_BM = 256
ROWMOVE_TB = 256
ATTN_SKIP_LOG = 88.0


def _dot(a, b):
    return jnp.dot(a, b, preferred_element_type=F32)


def _dot_nt(a, b):
    return lax.dot_general(a, b, (((1,), (1,)), ((), ())), preferred_element_type=F32)


def _rms_scale(x):
    return x * lax.rsqrt(jnp.mean(x * x, axis=-1, keepdims=True) + EPS)


def _gelu(x):
    c = math.sqrt(2.0 / math.pi)
    return 0.5 * x * (1.0 + jnp.tanh(c * (x + 0.044715 * (x * x * x))))


def _sigmoid(x):
    return 0.5 * (jnp.tanh(0.5 * x) + 1.0)


def _params(semantics, vmem_mib):
    return pltpu.CompilerParams(dimension_semantics=semantics,
                                vmem_limit_bytes=vmem_mib * MIB)


def _inproj_kernel(x_ref, g_ref, w_ref, ggm_ref, qkv_ref, u_ref, vn_ref, gates_ref, hn_ref):
    j = pl.program_id(1)

    @pl.when(j == 0)
    def _():
        hn_ref[...] = (_rms_scale(x_ref[...]) * g_ref[...]).astype(BF16)

    def acc():
        return _dot(hn_ref[...], w_ref[...])

    @pl.when(j < 3)
    def _():
        qkv_ref[...] = acc().astype(BF16)

    @pl.when(j == 3)
    def _():
        u_ref[...] = _gelu(acc()).astype(BF16)

    @pl.when(j == 4)
    def _():
        vn_ref[...] = (_rms_scale(_gelu(acc())) * ggm_ref[...]).astype(BF16)

    @pl.when(j >= 5)
    def _():
        gates_ref[...] = _sigmoid(acc()).astype(BF16)


def _inproj(x2d, g_mix, w_in_bf, g_gm):
    t, d = x2d.shape
    bm, bn = INPROJ_BM, INPROJ_BN
    ncol = w_in_bf.shape[1] // bn
    assert ncol == 9 and t % bm == 0
    out_shape = (
        jax.ShapeDtypeStruct((t, 3 * bn), BF16),
        jax.ShapeDtypeStruct((t, bn), BF16),
        jax.ShapeDtypeStruct((t, bn), BF16),
        jax.ShapeDtypeStruct((t, 4 * bn), BF16),
    )
    return pl.pallas_call(
        _inproj_kernel,
        out_shape=out_shape,
        grid=(t // bm, ncol),
        in_specs=[
            pl.BlockSpec((bm, d), lambda i, j: (i, 0)),
            pl.BlockSpec((1, d), lambda i, j: (0, 0)),
            pl.BlockSpec((d, bn), lambda i, j: (0, j)),
            pl.BlockSpec((1, bn), lambda i, j: (0, 0)),
        ],
        out_specs=(
            pl.BlockSpec((bm, bn), lambda i, j: (i, jnp.minimum(j, 2))),
            pl.BlockSpec((bm, bn), lambda i, j: (i, 0)),
            pl.BlockSpec((bm, bn), lambda i, j: (i, 0)),
            pl.BlockSpec((bm, bn), lambda i, j: (i, jnp.clip(j - 5, 0, 3))),
        ),
        scratch_shapes=[pltpu.VMEM((bm, d), BF16)],
        compiler_params=_params(("parallel", "arbitrary"), 48),
        name="inproj",
    )(x2d, g_mix, w_in_bf, g_gm)


def _sb_attn_kernel(q_ref, k_ref, v_ref, o_ref, *, scale):
    s, _ = q_ref.shape
    qb = ATTN_QB
    row = lax.broadcasted_iota(I32, (qb, qb), 0)
    col = lax.broadcasted_iota(I32, (qb, qb), 1)
    strict = col < row
    above = (row > col).astype(BF16)

    def chunk(q, j, carry, diagonal):
        start = pl.multiple_of(j * qb, qb)
        kc = k_ref[pl.ds(start, qb), :]
        vc = v_ref[pl.ds(start, qb), :]
        z = _dot_nt(q, kc) * scale
        lp = jnp.log1p(jnp.exp(-jnp.abs(z)))
        log_sig = jnp.minimum(z, 0.0) - lp
        log_not = -jnp.maximum(z, 0.0) - lp
        if diagonal:
            log_not = jnp.where(strict, log_not, 0.0)
        hi = log_not.astype(BF16)
        mid = (log_not - hi.astype(F32)).astype(BF16)
        between = _dot(hi, above) + _dot(mid, above)
        a = jnp.exp(log_sig + between + carry)
        if diagonal:
            a = jnp.where(strict, a, 0.0)
        o = _dot(a.astype(BF16), vc)
        return o, carry + between[:, 0:1] + log_not[:, 0:1]

    def qblock(i, _):
        start = pl.multiple_of(i * qb, qb)
        q = q_ref[pl.ds(start, qb), :]
        o, carry = chunk(q, i, jnp.zeros((qb, 1), F32), True)

        def cond(st):
            j, _, _, cmax = st
            return jnp.logical_and(j >= 0, cmax > -ATTN_SKIP_LOG)

        def body(st):
            j, o, carry, _ = st
            o2, carry2 = chunk(q, j, carry, False)
            return j - 1, o + o2, carry2, jnp.max(carry2)

        _, o, _, _ = lax.while_loop(cond, body, (i - 1, o, carry, jnp.max(carry)))
        o_ref[pl.ds(start, qb), :] = o.astype(o_ref.dtype)
        return 0

    lax.fori_loop(0, s // qb, qblock, 0)


def _sb_attn(qkv, batch, seq):
    width = qkv.shape[1] // 3
    dh = width // SB_HEADS
    assert dh == LANES and seq % ATTN_QB == 0
    qkv3 = qkv.reshape(batch, seq, 3 * width)
    kern = functools.partial(_sb_attn_kernel, scale=1.0 / math.sqrt(dh))
    blk = (None, seq, dh)
    o = pl.pallas_call(
        kern,
        out_shape=jax.ShapeDtypeStruct((batch, seq, width), BF16),
        grid=(batch, SB_HEADS),
        in_specs=[
            pl.BlockSpec(blk, lambda b, h: (b, 0, h)),
            pl.BlockSpec(blk, lambda b, h: (b, 0, SB_HEADS + h)),
            pl.BlockSpec(blk, lambda b, h: (b, 0, 2 * SB_HEADS + h)),
        ],
        out_specs=pl.BlockSpec(blk, lambda b, h: (b, 0, h)),
        compiler_params=_params(("parallel", "parallel"), 48),
        name="sb_attn",
    )(qkv3, qkv3, qkv3)
    return o.reshape(batch * seq, width)


def _merge_kernel(osb_ref, u_ref, vn_ref, gsb_ref, ggm_ref, wsp_ref, bsp_ref, wsb_ref, wgm_ref,
                  out_ref, ogm_ref):
    bm = osb_ref.shape[0]
    c = GM_CHUNK
    row = lax.broadcasted_iota(I32, (c, c), 0)
    col = lax.broadcasted_iota(I32, (c, c), 1)
    causal = row >= col
    for g in range(GM_GROUPS):
        w = jnp.where(causal, wsp_ref[g], 0.0).astype(BF16)
        gs = slice(g * c, (g + 1) * c)
        for ch in range(bm // c):
            rs = slice(ch * c, (ch + 1) * c)
            mixed = _dot(w, vn_ref[rs, gs]) + bsp_ref[g]
            ogm_ref[rs, gs] = (u_ref[rs, gs].astype(F32) * mixed).astype(BF16)
    merged = (gsb_ref[...].astype(F32) * _dot(osb_ref[...], wsb_ref[...])
              + ggm_ref[...].astype(F32) * _dot(ogm_ref[...], wgm_ref[...]))
    out_ref[...] = merged.astype(BF16)


def _merge(o_sb, u, vn, gates, w_spatial, b_bcast, w_sb_bf, w_gm_bf):
    t, width = o_sb.shape
    d = w_sb_bf.shape[1]
    bm = MERGE_BM
    assert t % bm == 0 and bm % GM_CHUNK == 0 and width == GM_GROUPS * GM_CHUNK
    const3 = lambda i: (0, 0, 0)
    const2 = lambda i: (0, 0)
    return pl.pallas_call(
        _merge_kernel,
        out_shape=jax.ShapeDtypeStruct((t, d), BF16),
        grid=(t // bm,),
        in_specs=[
            pl.BlockSpec((bm, width), lambda i: (i, 0)),
            pl.BlockSpec((bm, width), lambda i: (i, 0)),
            pl.BlockSpec((bm, width), lambda i: (i, 0)),
            pl.BlockSpec((bm, d), lambda i: (i, 0)),
            pl.BlockSpec((bm, d), lambda i: (i, 1)),
            pl.BlockSpec(w_spatial.shape, const3),
            pl.BlockSpec(b_bcast.shape, const3),
            pl.BlockSpec(w_sb_bf.shape, const2),
            pl.BlockSpec(w_gm_bf.shape, const2),
        ],
        out_specs=pl.BlockSpec((bm, d), lambda i: (i, 0)),
        scratch_shapes=[pltpu.VMEM((bm, width), BF16)],
        compiler_params=_params(("parallel",), 48),
        name="merge",
    )(o_sb, u, vn, gates, gates, w_spatial, b_bcast, w_sb_bf, w_gm_bf)


def _memkv_kernel(mem_ref, g_ref, w_ref, kv_ref):
    mn = (_rms_scale(mem_ref[...]) * g_ref[...]).astype(BF16)
    kv_ref[...] = _dot(mn, w_ref[...]).astype(BF16)


def _memkv(mem2d, g_mem, w_xkv_bf):
    rows, d = mem2d.shape
    n = w_xkv_bf.shape[1]
    bm = 256
    assert rows % bm == 0
    return pl.pallas_call(
        _memkv_kernel,
        out_shape=jax.ShapeDtypeStruct((rows, n), BF16),
        grid=(rows // bm,),
        in_specs=[
            pl.BlockSpec((bm, d), lambda i: (i, 0)),
            pl.BlockSpec((1, d), lambda i: (0, 0)),
            pl.BlockSpec((d, n), lambda i: (0, 0)),
        ],
        out_specs=pl.BlockSpec((bm, n), lambda i: (i, 0)),
        compiler_params=_params(("parallel",), 32),
        name="memkv",
    )(mem2d, g_mem, w_xkv_bf)


def _cross_kernel(x_ref, m_ref, wout_ref, gc_ref, wxq_ref, kv_ref, wxo_ref, out_ref):
    xw = wxq_ref.shape[1]
    dh = xw // X_HEADS
    x1 = x_ref[...] + _dot(m_ref[...], wout_ref[...])
    hc = (_rms_scale(x1) * gc_ref[...]).astype(BF16)
    q = _dot(hc, wxq_ref[...]).astype(BF16)
    inv = 1.0 / math.sqrt(dh)
    heads = []
    for h in range(X_HEADS):
        kh = kv_ref[:, h * dh:(h + 1) * dh]
        vh = kv_ref[:, xw + h * dh: xw + (h + 1) * dh]
        s = _dot_nt(q[:, h * dh:(h + 1) * dh], kh) * inv
        e = jnp.exp(s - jnp.max(s, axis=-1, keepdims=True))
        p = e / jnp.sum(e, axis=-1, keepdims=True)
        heads.append(_dot(p.astype(BF16), vh).astype(BF16))
    o = jnp.concatenate(heads, axis=-1)
    out_ref[...] = x1 + _dot(o, wxo_ref[...])


def _cross(x2d, merged, w_out_bf, g_cross, w_xq_bf, kv, w_xo_bf, seq, n_mem):
    t, d = x2d.shape
    bm = CROSS_BM
    assert seq % bm == 0
    steps_per_batch = seq // bm
    const2 = lambda i: (0, 0)
    return pl.pallas_call(
        _cross_kernel,
        out_shape=jax.ShapeDtypeStruct((t, d), F32),
        grid=(t // bm,),
        in_specs=[
            pl.BlockSpec((bm, d), lambda i: (i, 0)),
            pl.BlockSpec((bm, d), lambda i: (i, 0)),
            pl.BlockSpec(w_out_bf.shape, const2),
            pl.BlockSpec((1, d), const2),
            pl.BlockSpec(w_xq_bf.shape, const2),
            pl.BlockSpec((n_mem, kv.shape[1]), lambda i: (i // steps_per_batch, 0)),
            pl.BlockSpec(w_xo_bf.shape, const2),
        ],
        out_specs=pl.BlockSpec((bm, d), lambda i: (i, 0)),
        compiler_params=_params(("parallel",), 48),
        name="cross",
    )(x2d, merged, w_out_bf, g_cross, w_xq_bf, kv, w_xo_bf)


def _router_kernel(x_ref, g_ref, whi_ref, wlo_ref, b_ref, route_ref):
    h = _rms_scale(x_ref[...]) * g_ref[...]
    h_hi = h.astype(BF16)
    h_lo = (h - h_hi.astype(F32)).astype(BF16)
    whi = whi_ref[...]
    lg = _dot(h_hi, whi) + _dot(h_hi, wlo_ref[...]) + _dot(h_lo, whi) + b_ref[...]
    lane = lax.broadcasted_iota(I32, lg.shape, 1)
    neg = -jnp.inf

    def first_max(mask):
        v = jnp.max(jnp.where(mask, lg, neg), axis=-1, keepdims=True)
        hit = jnp.logical_and(mask, lg == v)
        return v, jnp.min(jnp.where(hit, lane, LANES), axis=-1, keepdims=True)

    gmask = lane < N_GROUPS
    gmax, gidx = first_max(gmask)
    p_g = 1.0 / jnp.sum(jnp.where(gmask, jnp.exp(lg - gmax), 0.0), axis=-1, keepdims=True)
    lo = N_GROUPS + EXPERTS_PER_GROUP * gidx
    emask = jnp.logical_and(lane >= lo, lane < lo + EXPERTS_PER_GROUP)
    v1, i1 = first_max(emask)
    v2, i2 = first_max(jnp.logical_and(emask, lane != i1))
    t = jnp.exp(v2 - v1)
    w1 = p_g / (1.0 + t)
    w2 = w1 * t
    e1 = (i1 - N_GROUPS).astype(F32)
    e2 = (i2 - N_GROUPS).astype(F32)
    route_ref[...] = jnp.where(lane == 0, e1,
                               jnp.where(lane == 1, e2,
                                         jnp.where(lane == 2, w1,
                                                   jnp.where(lane == 3, w2, 0.0))))


def _router(x2d, g_ffn, w_hi, w_lo, b_r):
    t, d = x2d.shape
    bm = ROUTER_BM
    const2 = lambda i: (0, 0)
    return pl.pallas_call(
        _router_kernel,
        out_shape=jax.ShapeDtypeStruct((t, LANES), F32),
        grid=(t // bm,),
        in_specs=[
            pl.BlockSpec((bm, d), lambda i: (i, 0)),
            pl.BlockSpec((1, d), const2),
            pl.BlockSpec((d, LANES), const2),
            pl.BlockSpec((d, LANES), const2),
            pl.BlockSpec((1, LANES), const2),
        ],
        out_specs=pl.BlockSpec((bm, LANES), lambda i: (i, 0)),
        compiler_params=_params(("parallel",), 32),
        name="router",
    )(x2d, g_ffn, w_hi, w_lo, b_r)


def _rank_kernel(e_ref, rank_ref, cnt_ref, carry_ref):
    @pl.when(pl.program_id(0) == 0)
    def _():
        carry_ref[...] = jnp.zeros_like(carry_ref)

    ex = lax.broadcasted_iota(I32, (N_EXPERTS, LANES), 0)
    r = lax.broadcasted_iota(I32, (LANES, LANES), 0)
    c = lax.broadcasted_iota(I32, (LANES, LANES), 1)
    before = (r < c).astype(BF16)
    carry = carry_ref[...]
    for i in range(e_ref.shape[0]):
        hit = ex == e_ref[i:i + 1, :]
        onehot = jnp.where(hit, 1.0, 0.0)
        prefix = _dot(onehot.astype(BF16), before) + carry
        rank_ref[i:i + 1, :] = jnp.sum(jnp.where(hit, prefix, 0.0), axis=0, keepdims=True).astype(I32)
        carry = carry + jnp.sum(onehot, axis=1, keepdims=True)
    carry_ref[...] = carry
    cnt_ref[...] = carry


def _rank(e2d):
    rows = e2d.shape[0]
    rr = RANK_ROWS
    assert rows % rr == 0
    return pl.pallas_call(
        _rank_kernel,
        out_shape=(jax.ShapeDtypeStruct((rows, LANES), I32),
                   jax.ShapeDtypeStruct((N_EXPERTS, LANES), F32)),
        grid=(rows // rr,),
        in_specs=[pl.BlockSpec((rr, LANES), lambda i: (i, 0))],
        out_specs=(pl.BlockSpec((rr, LANES), lambda i: (i, 0)),
                   pl.BlockSpec((N_EXPERTS, LANES), lambda i: (0, 0))),
        scratch_shapes=[pltpu.VMEM((N_EXPERTS, LANES), F32)],
        compiler_params=_params(("arbitrary",), 16),
        name="rank",
    )(e2d)


def _dispatch_kernel(tail_ref, pend_ref, dest_ref, x_hbm, xs_hbm, zero_ref, sem):
    i = pl.program_id(0)
    tb = ROWMOVE_TB
    sub = zero_ref.shape[0]

    @pl.when(i == 0)
    def _():
        zero_ref[...] = jnp.zeros_like(zero_ref)

        def fill(e, _):
            tail = tail_ref[e]
            pend = pend_ref[e]
            aligned = jnp.bitwise_and(tail + (sub - 1), -sub)
            n_tiles = lax.shift_right_logical(pend - aligned, int(math.log2(sub)))

            def row_fill(r):
                return pltpu.make_async_copy(zero_ref.at[pl.ds(0, 1)],
                                             xs_hbm.at[pl.ds(tail + r, 1)], sem)

            def tile_fill(b):
                start = pl.multiple_of(aligned + b * sub, sub)
                return pltpu.make_async_copy(zero_ref, xs_hbm.at[pl.ds(start, sub)], sem)

            for r in range(sub - 1):
                pl.when(tail + r < aligned)(lambda r=r: row_fill(r).start())
            lax.fori_loop(0, n_tiles, lambda b, c: (tile_fill(b).start(), c)[1], 0)
            for r in range(sub - 1):
                pl.when(tail + r < aligned)(lambda r=r: row_fill(r).wait())
            lax.fori_loop(0, n_tiles, lambda b, c: (tile_fill(b).wait(), c)[1], 0)
            return 0

        lax.fori_loop(0, N_EXPERTS, fill, 0)

    def row_copy(t, k):
        return pltpu.make_async_copy(x_hbm.at[pl.ds(i * tb + t, 1)],
                                     xs_hbm.at[pl.ds(dest_ref[0, 0, TOP_K * t + k], 1)], sem)

    def start(t, _):
        for k in range(TOP_K):
            row_copy(t, k).start()
        return 0

    def wait(t, _):
        for k in range(TOP_K):
            row_copy(t, k).wait()
        return 0

    lax.fori_loop(0, tb, start, 0)
    lax.fori_loop(0, tb, wait, 0)


def _dispatch(x2d, dest3, tails, pends, n_rows):
    t, d = x2d.shape
    tb = ROWMOVE_TB
    return pl.pallas_call(
        _dispatch_kernel,
        out_shape=jax.ShapeDtypeStruct((n_rows, d), F32),
        grid_spec=pltpu.PrefetchScalarGridSpec(
            num_scalar_prefetch=2,
            grid=(t // tb,),
            in_specs=[
                pl.BlockSpec((1, 1, TOP_K * tb), lambda i, tl, pe: (i, 0, 0),
                             memory_space=pltpu.SMEM),
                pl.BlockSpec(memory_space=pl.ANY),
            ],
            out_specs=pl.BlockSpec(memory_space=pl.ANY),
            scratch_shapes=[pltpu.VMEM((F32_SUBLANES, d), F32), pltpu.SemaphoreType.DMA(())],
        ),
        compiler_params=_params(("arbitrary",), 16),
        name="dispatch",
    )(tails, pends, dest3, x2d)


def _moe_kernel(be_ref, nu_ref, xs_ref, g_ref, wg_ref, wu_ref, wd_ref, ys_ref):
    @pl.when(pl.program_id(0) < nu_ref[0])
    def _():
        x = (_rms_scale(xs_ref[...]) * g_ref[...]).astype(BF16)
        a = _dot(x, wg_ref[...])
        b = _dot(x, wu_ref[...])
        h = (a * _sigmoid(a) * b).astype(BF16)
        ys_ref[...] = _dot(h, wd_ref[...])


def _moe(xs, g_ffn, wg_bf, wu_bf, wd_bf, blk_e, n_used, n_blocks):
    d = xs.shape[1]
    de = wg_bf.shape[2]
    bm = MOE_BM
    used = lambda i, be, nu: (jnp.minimum(i, nu[0] - 1), 0)
    return pl.pallas_call(
        _moe_kernel,
        out_shape=jax.ShapeDtypeStruct((n_blocks * bm, d), F32),
        grid_spec=pltpu.PrefetchScalarGridSpec(
            num_scalar_prefetch=2,
            grid=(n_blocks,),
            in_specs=[
                pl.BlockSpec((bm, d), used),
                pl.BlockSpec((1, d), lambda i, be, nu: (0, 0)),
                pl.BlockSpec((None, d, de), lambda i, be, nu: (be[i], 0, 0)),
                pl.BlockSpec((None, d, de), lambda i, be, nu: (be[i], 0, 0)),
                pl.BlockSpec((None, de, d), lambda i, be, nu: (be[i], 0, 0)),
            ],
            out_specs=pl.BlockSpec((bm, d), used),
        ),
        compiler_params=_params(("arbitrary",), 48),
        name="moe",
    )(blk_e, n_used, xs, g_ffn, wg_bf, wu_bf, wd_bf)


def _combine_kernel(dest_ref, x_ref, route_ref, g_ref, ys_hbm, out_ref, ybuf, sem):
    tb = ROWMOVE_TB

    def row_copy(t, k):
        return pltpu.make_async_copy(ys_hbm.at[pl.ds(dest_ref[0, 0, TOP_K * t + k], 1)],
                                     ybuf.at[k, pl.ds(t, 1)], sem)

    def start(t, _):
        for k in range(TOP_K):
            row_copy(t, k).start()
        return 0

    def wait(t, _):
        for k in range(TOP_K):
            row_copy(t, k).wait()
        return 0

    lax.fori_loop(0, tb, start, 0)
    lax.fori_loop(0, tb, wait, 0)
    w1 = route_ref[:, 2:3]
    w2 = route_ref[:, 3:4]
    x3 = x_ref[...] + (w1 * ybuf[0] + w2 * ybuf[1])
    out_ref[...] = _rms_scale(x3) * g_ref[...]


def _combine(x2d, route, g_final, ys, dest3):
    t, d = x2d.shape
    tb = ROWMOVE_TB
    return pl.pallas_call(
        _combine_kernel,
        out_shape=jax.ShapeDtypeStruct((t, d), F32),
        grid=(t // tb,),
        in_specs=[
            pl.BlockSpec((1, 1, TOP_K * tb), lambda i: (i, 0, 0), memory_space=pltpu.SMEM),
            pl.BlockSpec((tb, d), lambda i: (i, 0)),
            pl.BlockSpec((tb, LANES), lambda i: (i, 0)),
            pl.BlockSpec((1, d), lambda i: (0, 0)),
            pl.BlockSpec(memory_space=pl.ANY),
        ],
        out_specs=pl.BlockSpec((tb, d), lambda i: (i, 0)),
        scratch_shapes=[pltpu.VMEM((TOP_K, tb, d), F32), pltpu.SemaphoreType.DMA(())],
        compiler_params=_params(("arbitrary",), 32),
        name="combine",
    )(dest3, x2d, route, g_final, ys)


def _layer(x, mem, g_mix, w_in, g_gm, w_spatial, b_spatial, w_branch_sb, w_branch_gm, w_out,
           g_cross, g_mem, w_xq, w_xkv, w_xo, g_ffn, w_rg, b_rg, w_re, b_re,
           w_e_gate, w_e_up, w_e_down, g_final):
    batch, seq, d = x.shape
    n_mem = mem.shape[1]
    t = batch * seq
    row = lambda v: v.reshape(1, -1)
    x2d = x.reshape(t, d)

    qkv, u, vn, gates = _inproj(x2d, row(g_mix), w_in.astype(BF16), row(g_gm))
    o_sb = _sb_attn(qkv, batch, seq)
    b_bcast = jnp.broadcast_to(b_spatial[:, :, None], b_spatial.shape + (GM_CHUNK,))
    merged = _merge(o_sb, u, vn, gates, w_spatial, b_bcast,
                    w_branch_sb.astype(BF16), w_branch_gm.astype(BF16))
    kv = _memkv(mem.reshape(batch * n_mem, d), row(g_mem), w_xkv.astype(BF16))
    x2 = _cross(x2d, merged, w_out.astype(BF16), row(g_cross), w_xq.astype(BF16), kv,
                w_xo.astype(BF16), seq, n_mem)

    pad = LANES - N_GROUPS - N_EXPERTS
    w_r = jnp.concatenate([w_rg, w_re, jnp.zeros((d, pad), F32)], axis=1)
    b_r = jnp.concatenate([b_rg, b_re, jnp.zeros((pad,), F32)]).reshape(1, LANES)
    w_r_hi = w_r.astype(BF16)
    w_r_lo = (w_r - w_r_hi.astype(F32)).astype(BF16)
    route = _router(x2, row(g_ffn), w_r_hi, w_r_lo, b_r)

    n = t * TOP_K
    e_flat = route[:, :TOP_K].astype(I32).reshape(n)
    rank2d, cnt = _rank(e_flat.reshape(n // LANES, LANES))
    counts = cnt[:, 0].astype(I32)
    padded = ((counts + MOE_BM - 1) // MOE_BM) * MOE_BM
    pends = jnp.cumsum(padded)
    pstarts = pends - padded
    dest = pstarts[e_flat] + rank2d.reshape(n)
    n_blocks = n // MOE_BM + N_EXPERTS
    blk_start = jnp.arange(n_blocks, dtype=I32) * MOE_BM
    blk_e = jnp.minimum(jnp.sum(pends[None, :] <= blk_start[:, None], axis=1),
                        N_EXPERTS - 1).astype(I32)
    n_used = (pends[-1:] // MOE_BM).astype(I32)
    dest3 = dest.reshape(t // ROWMOVE_TB, 1, TOP_K * ROWMOVE_TB)

    xs = _dispatch(x2, dest3, (pstarts + counts).astype(I32), pends.astype(I32),
                   n_blocks * MOE_BM)
    ys = _moe(xs, row(g_ffn), w_e_gate.astype(BF16), w_e_up.astype(BF16), w_e_down.astype(BF16),
              blk_e, n_used, n_blocks)
    out = _combine(x2, route, row(g_final), ys, dest3)
    return out.reshape(batch, seq, d)


def kernel(x, mem, g_mix, w_in, g_gm, w_spatial, b_spatial, w_branch_sb, w_branch_gm, w_out,
           g_cross, g_mem, w_xq, w_xkv, w_xo, g_ffn, w_rg, b_rg, w_re, b_re,
           w_e_gate, w_e_up, w_e_down, g_final):
    assert w_in.shape[0] == 1, "single layer"
    return _layer(x, mem, g_mix[0], w_in[0], g_gm[0], w_spatial[0], b_spatial[0],
                  w_branch_sb[0], w_branch_gm[0], w_out[0], g_cross[0], g_mem[0], w_xq[0],
                  w_xkv[0], w_xo[0], g_ffn[0], w_rg[0], b_rg[0], w_re[0], b_re[0],
                  w_e_gate[0], w_e_up[0], w_e_down[0], g_final)
```

```python
import functools
import math

import jax
import jax.numpy as jnp
from jax import lax
from jax.experimental import pallas as pl
from jax.experimental.pallas import tpu as pltpu

F32 = jnp.float32
BF16 = jnp.bfloat16
I32 = jnp.int32

EPS = 1e-6
LANES = 128
F32_SUBLANES = 8
MIB = 1024 * 1024

SB_HEADS = 8
GM_GROUPS = 8
GM_CHUNK = 128
X_HEADS = 4
N_GROUPS = 4
EXPERTS_PER_GROUP = 8
N_EXPERTS = N_GROUPS * EXPERTS_PER_GROUP
TOP_K = 2

INPROJ_BM = 512
INPROJ_BN = 1024
ATTN_QB = 128
ATTN_CHAINS = 4
MERGE_BM = 256
CROSS_BM = 256
ROUTER_BM = 512
RANK_ROWS = 8
MOE_BM = 256
ROWMOVE_TB = 256
ATTN_SKIP_LOG = 88.0


def _dot(a, b):
    return jnp.dot(a, b, preferred_element_type=F32)


def _dot_nt(a, b):
    return lax.dot_general(a, b, (((1,), (1,)), ((), ())), preferred_element_type=F32)


def _rms_scale(x):
    return x * lax.rsqrt(jnp.mean(x * x, axis=-1, keepdims=True) + EPS)


def _gelu(x):
    c = math.sqrt(2.0 / math.pi)
    return 0.5 * x * (1.0 + jnp.tanh(c * (x + 0.044715 * (x * x * x))))


def _sigmoid(x):
    return 0.5 * (jnp.tanh(0.5 * x) + 1.0)


def _params(semantics, vmem_mib):
    return pltpu.CompilerParams(dimension_semantics=semantics,
                                vmem_limit_bytes=vmem_mib * MIB)


def _inproj_kernel(x_ref, g_ref, w_ref, ggm_ref, qkv_ref, u_ref, vn_ref, gates_ref, hn_ref):
    j = pl.program_id(1)

    @pl.when(j == 0)
    def _():
        hn_ref[...] = (_rms_scale(x_ref[...]) * g_ref[...]).astype(BF16)

    def acc():
        return _dot(hn_ref[...], w_ref[...])

    @pl.when(j < 3)
    def _():
        qkv_ref[...] = acc().astype(BF16)

    @pl.when(j == 3)
    def _():
        u_ref[...] = _gelu(acc()).astype(BF16)

    @pl.when(j == 4)
    def _():
        vn_ref[...] = (_rms_scale(_gelu(acc())) * ggm_ref[...]).astype(BF16)

    @pl.when(j >= 5)
    def _():
        gates_ref[...] = _sigmoid(acc()).astype(BF16)


def _inproj(x2d, g_mix, w_in_bf, g_gm):
    t, d = x2d.shape
    bm, bn = INPROJ_BM, INPROJ_BN
    ncol = w_in_bf.shape[1] // bn
    assert ncol == 9 and t % bm == 0
    out_shape = (
        jax.ShapeDtypeStruct((t, 3 * bn), BF16),
        jax.ShapeDtypeStruct((t, bn), BF16),
        jax.ShapeDtypeStruct((t, bn), BF16),
        jax.ShapeDtypeStruct((t, 4 * bn), BF16),
    )
    return pl.pallas_call(
        _inproj_kernel,
        out_shape=out_shape,
        grid=(t // bm, ncol),
        in_specs=[
            pl.BlockSpec((bm, d), lambda i, j: (i, 0)),
            pl.BlockSpec((1, d), lambda i, j: (0, 0)),
            pl.BlockSpec((d, bn), lambda i, j: (0, j)),
            pl.BlockSpec((1, bn), lambda i, j: (0, 0)),
        ],
        out_specs=(
            pl.BlockSpec((bm, bn), lambda i, j: (i, jnp.minimum(j, 2))),
            pl.BlockSpec((bm, bn), lambda i, j: (i, 0)),
            pl.BlockSpec((bm, bn), lambda i, j: (i, 0)),
            pl.BlockSpec((bm, bn), lambda i, j: (i, jnp.clip(j - 5, 0, 3))),
        ),
        scratch_shapes=[pltpu.VMEM((bm, d), BF16)],
        compiler_params=_params(("parallel", "arbitrary"), 48),
        name="inproj",
    )(x2d, g_mix, w_in_bf, g_gm)


def _sb_attn_kernel(q_ref, k_ref, v_ref, o_ref, *, scale):
    s, _ = q_ref.shape
    qb = ATTN_QB
    row = lax.broadcasted_iota(I32, (qb, qb), 0)
    col = lax.broadcasted_iota(I32, (qb, qb), 1)
    strict = col < row
    above = (row > col).astype(BF16)

    def chunk(q, j, carry, diagonal):
        start = pl.multiple_of(j * qb, qb)
        kc = k_ref[pl.ds(start, qb), :]
        vc = v_ref[pl.ds(start, qb), :]
        z = _dot_nt(q, kc) * scale
        lp = jnp.log1p(jnp.exp(-jnp.abs(z)))
        log_sig = jnp.minimum(z, 0.0) - lp
        log_not = -jnp.maximum(z, 0.0) - lp
        if diagonal:
            log_not = jnp.where(strict, log_not, 0.0)
        hi = log_not.astype(BF16)
        mid = (log_not - hi.astype(F32)).astype(BF16)
        between = _dot(hi, above) + _dot(mid, above)
        a = jnp.exp(log_sig + between + carry)
        if diagonal:
            a = jnp.where(strict, a, 0.0)
        o = _dot(a.astype(BF16), vc)
        return o, carry + between[:, 0:1] + log_not[:, 0:1]

    chains = ATTN_CHAINS
    per_chain = s // qb // chains

    def qgroup(g, _):
        blocks = [c * per_chain + g for c in range(chains)]
        qs = [q_ref[pl.ds(pl.multiple_of(i * qb, qb), qb), :] for i in blocks]
        first = [chunk(q, i, jnp.zeros((qb, 1), F32), True) for q, i in zip(qs, blocks)]

        def pending(step, carries):
            need = [jnp.where(i - step >= 0, jnp.max(c), -jnp.inf) for i, c in zip(blocks, carries)]
            return functools.reduce(jnp.maximum, need)

        def cond(st):
            return st[3] > -ATTN_SKIP_LOG

        def body(st):
            step, outs, carries, _ = st
            new_outs, new_carries = [], []
            for q, i, o, carry in zip(qs, blocks, outs, carries):
                j = i - step
                o2, carry2 = chunk(q, jnp.maximum(j, 0), carry, False)
                new_outs.append(o + jnp.where(j >= 0, o2, 0.0))
                new_carries.append(carry2)
            return step + 1, tuple(new_outs), tuple(new_carries), pending(step + 1, new_carries)

        outs = tuple(o for o, _ in first)
        carries = tuple(c for _, c in first)
        _, outs, _, _ = lax.while_loop(cond, body, (1, outs, carries, pending(1, carries)))
        for i, o in zip(blocks, outs):
            o_ref[pl.ds(pl.multiple_of(i * qb, qb), qb), :] = o.astype(o_ref.dtype)
        return 0

    lax.fori_loop(0, per_chain, qgroup, 0)


def _sb_attn(qkv, batch, seq):
    width = qkv.shape[1] // 3
    dh = width // SB_HEADS
    assert dh == LANES and seq % (ATTN_QB * ATTN_CHAINS) == 0
    qkv3 = qkv.reshape(batch, seq, 3 * width)
    kern = functools.partial(_sb_attn_kernel, scale=1.0 / math.sqrt(dh))
    blk = (None, seq, dh)
    o = pl.pallas_call(
        kern,
        out_shape=jax.ShapeDtypeStruct((batch, seq, width), BF16),
        grid=(batch, SB_HEADS),
        in_specs=[
            pl.BlockSpec(blk, lambda b, h: (b, 0, h)),
            pl.BlockSpec(blk, lambda b, h: (b, 0, SB_HEADS + h)),
            pl.BlockSpec(blk, lambda b, h: (b, 0, 2 * SB_HEADS + h)),
        ],
        out_specs=pl.BlockSpec(blk, lambda b, h: (b, 0, h)),
        compiler_params=_params(("parallel", "parallel"), 48),
        name="sb_attn",
    )(qkv3, qkv3, qkv3)
    return o.reshape(batch * seq, width)


def _merge_kernel(osb_ref, u_ref, vn_ref, gsb_ref, ggm_ref, wsp_ref, bsp_ref, wsb_ref, wgm_ref,
                  out_ref, ogm_ref):
    bm = osb_ref.shape[0]
    c = GM_CHUNK
    row = lax.broadcasted_iota(I32, (c, c), 0)
    col = lax.broadcasted_iota(I32, (c, c), 1)
    causal = row >= col
    for g in range(GM_GROUPS):
        w = jnp.where(causal, wsp_ref[g], 0.0).astype(BF16)
        gs = slice(g * c, (g + 1) * c)
        for ch in range(bm // c):
            rs = slice(ch * c, (ch + 1) * c)
            mixed = _dot(w, vn_ref[rs, gs]) + bsp_ref[g]
            ogm_ref[rs, gs] = (u_ref[rs, gs].astype(F32) * mixed).astype(BF16)
    merged = (gsb_ref[...].astype(F32) * _dot(osb_ref[...], wsb_ref[...])
              + ggm_ref[...].astype(F32) * _dot(ogm_ref[...], wgm_ref[...]))
    out_ref[...] = merged.astype(BF16)


def _merge(o_sb, u, vn, gates, w_spatial, b_bcast, w_sb_bf, w_gm_bf):
    t, width = o_sb.shape
    d = w_sb_bf.shape[1]
    bm = MERGE_BM
    assert t % bm == 0 and bm % GM_CHUNK == 0 and width == GM_GROUPS * GM_CHUNK
    const3 = lambda i: (0, 0, 0)
    const2 = lambda i: (0, 0)
    return pl.pallas_call(
        _merge_kernel,
        out_shape=jax.ShapeDtypeStruct((t, d), BF16),
        grid=(t // bm,),
        in_specs=[
            pl.BlockSpec((bm, width), lambda i: (i, 0)),
            pl.BlockSpec((bm, width), lambda i: (i, 0)),
            pl.BlockSpec((bm, width), lambda i: (i, 0)),
            pl.BlockSpec((bm, d), lambda i: (i, 0)),
            pl.BlockSpec((bm, d), lambda i: (i, 1)),
            pl.BlockSpec(w_spatial.shape, const3),
            pl.BlockSpec(b_bcast.shape, const3),
            pl.BlockSpec(w_sb_bf.shape, const2),
            pl.BlockSpec(w_gm_bf.shape, const2),
        ],
        out_specs=pl.BlockSpec((bm, d), lambda i: (i, 0)),
        scratch_shapes=[pltpu.VMEM((bm, width), BF16)],
        compiler_params=_params(("parallel",), 48),
        name="merge",
    )(o_sb, u, vn, gates, gates, w_spatial, b_bcast, w_sb_bf, w_gm_bf)


def _memkv_kernel(mem_ref, g_ref, w_ref, kv_ref):
    mn = (_rms_scale(mem_ref[...]) * g_ref[...]).astype(BF16)
    kv_ref[...] = _dot(mn, w_ref[...]).astype(BF16)


def _memkv(mem2d, g_mem, w_xkv_bf):
    rows, d = mem2d.shape
    n = w_xkv_bf.shape[1]
    bm = 256
    assert rows % bm == 0
    return pl.pallas_call(
        _memkv_kernel,
        out_shape=jax.ShapeDtypeStruct((rows, n), BF16),
        grid=(rows // bm,),
        in_specs=[
            pl.BlockSpec((bm, d), lambda i: (i, 0)),
            pl.BlockSpec((1, d), lambda i: (0, 0)),
            pl.BlockSpec((d, n), lambda i: (0, 0)),
        ],
        out_specs=pl.BlockSpec((bm, n), lambda i: (i, 0)),
        compiler_params=_params(("parallel",), 32),
        name="memkv",
    )(mem2d, g_mem, w_xkv_bf)


def _cross_kernel(x_ref, m_ref, wout_ref, gc_ref, wxq_ref, kv_ref, wxo_ref, out_ref):
    xw = wxq_ref.shape[1]
    dh = xw // X_HEADS
    x1 = x_ref[...] + _dot(m_ref[...], wout_ref[...])
    hc = (_rms_scale(x1) * gc_ref[...]).astype(BF16)
    q = _dot(hc, wxq_ref[...]).astype(BF16)
    inv = 1.0 / math.sqrt(dh)
    heads = []
    for h in range(X_HEADS):
        kh = kv_ref[:, h * dh:(h + 1) * dh]
        vh = kv_ref[:, xw + h * dh: xw + (h + 1) * dh]
        s = _dot_nt(q[:, h * dh:(h + 1) * dh], kh) * inv
        e = jnp.exp(s - jnp.max(s, axis=-1, keepdims=True))
        p = e / jnp.sum(e, axis=-1, keepdims=True)
        heads.append(_dot(p.astype(BF16), vh).astype(BF16))
    o = jnp.concatenate(heads, axis=-1)
    out_ref[...] = x1 + _dot(o, wxo_ref[...])


def _cross(x2d, merged, w_out_bf, g_cross, w_xq_bf, kv, w_xo_bf, seq, n_mem):
    t, d = x2d.shape
    bm = CROSS_BM
    assert seq % bm == 0
    steps_per_batch = seq // bm
    const2 = lambda i: (0, 0)
    return pl.pallas_call(
        _cross_kernel,
        out_shape=jax.ShapeDtypeStruct((t, d), F32),
        grid=(t // bm,),
        in_specs=[
            pl.BlockSpec((bm, d), lambda i: (i, 0)),
            pl.BlockSpec((bm, d), lambda i: (i, 0)),
            pl.BlockSpec(w_out_bf.shape, const2),
            pl.BlockSpec((1, d), const2),
            pl.BlockSpec(w_xq_bf.shape, const2),
            pl.BlockSpec((n_mem, kv.shape[1]), lambda i: (i // steps_per_batch, 0)),
            pl.BlockSpec(w_xo_bf.shape, const2),
        ],
        out_specs=pl.BlockSpec((bm, d), lambda i: (i, 0)),
        compiler_params=_params(("parallel",), 48),
        name="cross",
    )(x2d, merged, w_out_bf, g_cross, w_xq_bf, kv, w_xo_bf)


def _router_kernel(x_ref, g_ref, whi_ref, wlo_ref, b_ref, route_ref):
    h = _rms_scale(x_ref[...]) * g_ref[...]
    h_hi = h.astype(BF16)
    h_lo = (h - h_hi.astype(F32)).astype(BF16)
    whi = whi_ref[...]
    lg = _dot(h_hi, whi) + _dot(h_hi, wlo_ref[...]) + _dot(h_lo, whi) + b_ref[...]
    lane = lax.broadcasted_iota(I32, lg.shape, 1)
    neg = -jnp.inf

    def first_max(mask):
        v = jnp.max(jnp.where(mask, lg, neg), axis=-1, keepdims=True)
        hit = jnp.logical_and(mask, lg == v)
        return v, jnp.min(jnp.where(hit, lane, LANES), axis=-1, keepdims=True)

    gmask = lane < N_GROUPS
    gmax, gidx = first_max(gmask)
    p_g = 1.0 / jnp.sum(jnp.where(gmask, jnp.exp(lg - gmax), 0.0), axis=-1, keepdims=True)
    lo = N_GROUPS + EXPERTS_PER_GROUP * gidx
    emask = jnp.logical_and(lane >= lo, lane < lo + EXPERTS_PER_GROUP)
    v1, i1 = first_max(emask)
    v2, i2 = first_max(jnp.logical_and(emask, lane != i1))
    t = jnp.exp(v2 - v1)
    w1 = p_g / (1.0 + t)
    w2 = w1 * t
    e1 = (i1 - N_GROUPS).astype(F32)
    e2 = (i2 - N_GROUPS).astype(F32)
    route_ref[...] = jnp.where(lane == 0, e1,
                               jnp.where(lane == 1, e2,
                                         jnp.where(lane == 2, w1,
                                                   jnp.where(lane == 3, w2, 0.0))))


def _router(x2d, g_ffn, w_hi, w_lo, b_r):
    t, d = x2d.shape
    bm = ROUTER_BM
    const2 = lambda i: (0, 0)
    return pl.pallas_call(
        _router_kernel,
        out_shape=jax.ShapeDtypeStruct((t, LANES), F32),
        grid=(t // bm,),
        in_specs=[
            pl.BlockSpec((bm, d), lambda i: (i, 0)),
            pl.BlockSpec((1, d), const2),
            pl.BlockSpec((d, LANES), const2),
            pl.BlockSpec((d, LANES), const2),
            pl.BlockSpec((1, LANES), const2),
        ],
        out_specs=pl.BlockSpec((bm, LANES), lambda i: (i, 0)),
        compiler_params=_params(("parallel",), 32),
        name="router",
    )(x2d, g_ffn, w_hi, w_lo, b_r)


def _rank_kernel(e_ref, rank_ref, cnt_ref, carry_ref):
    @pl.when(pl.program_id(0) == 0)
    def _():
        carry_ref[...] = jnp.zeros_like(carry_ref)

    ex = lax.broadcasted_iota(I32, (N_EXPERTS, LANES), 0)
    r = lax.broadcasted_iota(I32, (LANES, LANES), 0)
    c = lax.broadcasted_iota(I32, (LANES, LANES), 1)
    before = (r < c).astype(BF16)
    carry = carry_ref[...]
    for i in range(e_ref.shape[0]):
        hit = ex == e_ref[i:i + 1, :]
        onehot = jnp.where(hit, 1.0, 0.0)
        prefix = _dot(onehot.astype(BF16), before) + carry
        rank_ref[i:i + 1, :] = jnp.sum(jnp.where(hit, prefix, 0.0), axis=0, keepdims=True).astype(I32)
        carry = carry + jnp.sum(onehot, axis=1, keepdims=True)
    carry_ref[...] = carry
    cnt_ref[...] = carry


def _rank(e2d):
    rows = e2d.shape[0]
    rr = RANK_ROWS
    assert rows % rr == 0
    return pl.pallas_call(
        _rank_kernel,
        out_shape=(jax.ShapeDtypeStruct((rows, LANES), I32),
                   jax.ShapeDtypeStruct((N_EXPERTS, LANES), F32)),
        grid=(rows // rr,),
        in_specs=[pl.BlockSpec((rr, LANES), lambda i: (i, 0))],
        out_specs=(pl.BlockSpec((rr, LANES), lambda i: (i, 0)),
                   pl.BlockSpec((N_EXPERTS, LANES), lambda i: (0, 0))),
        scratch_shapes=[pltpu.VMEM((N_EXPERTS, LANES), F32)],
        compiler_params=_params(("arbitrary",), 16),
        name="rank",
    )(e2d)


def _dispatch_kernel(tail_ref, pend_ref, dest_ref, x_ref, xs_hbm, zero_ref, sem):
    i = pl.program_id(0)
    tb = ROWMOVE_TB
    sub = zero_ref.shape[0]

    @pl.when(i == 0)
    def _():
        zero_ref[...] = jnp.zeros_like(zero_ref)

        def fill(e, _):
            tail = tail_ref[e]
            pend = pend_ref[e]
            aligned = jnp.bitwise_and(tail + (sub - 1), -sub)
            n_tiles = lax.shift_right_logical(pend - aligned, int(math.log2(sub)))

            def row_fill(r):
                return pltpu.make_async_copy(zero_ref.at[pl.ds(0, 1)],
                                             xs_hbm.at[pl.ds(tail + r, 1)], sem)

            def tile_fill(b):
                start = pl.multiple_of(aligned + b * sub, sub)
                return pltpu.make_async_copy(zero_ref, xs_hbm.at[pl.ds(start, sub)], sem)

            for r in range(sub - 1):
                pl.when(tail + r < aligned)(lambda r=r: row_fill(r).start())
            lax.fori_loop(0, n_tiles, lambda b, c: (tile_fill(b).start(), c)[1], 0)
            for r in range(sub - 1):
                pl.when(tail + r < aligned)(lambda r=r: row_fill(r).wait())
            lax.fori_loop(0, n_tiles, lambda b, c: (tile_fill(b).wait(), c)[1], 0)
            return 0

        lax.fori_loop(0, N_EXPERTS, fill, 0)

    def row_copy(t, k):
        return pltpu.make_async_copy(x_ref.at[pl.ds(t, 1)],
                                     xs_hbm.at[pl.ds(dest_ref[0, 0, TOP_K * t + k], 1)], sem)

    def start(t, _):
        for k in range(TOP_K):
            row_copy(t, k).start()
        return 0

    def wait(t, _):
        for k in range(TOP_K):
            row_copy(t, k).wait()
        return 0

    lax.fori_loop(0, tb, start, 0)
    lax.fori_loop(0, tb, wait, 0)


def _dispatch(x2d, dest3, tails, pends, n_rows):
    t, d = x2d.shape
    tb = ROWMOVE_TB
    return pl.pallas_call(
        _dispatch_kernel,
        out_shape=jax.ShapeDtypeStruct((n_rows, d), F32),
        grid_spec=pltpu.PrefetchScalarGridSpec(
            num_scalar_prefetch=2,
            grid=(t // tb,),
            in_specs=[
                pl.BlockSpec((1, 1, TOP_K * tb), lambda i, tl, pe: (i, 0, 0),
                             memory_space=pltpu.SMEM),
                pl.BlockSpec((tb, d), lambda i, tl, pe: (i, 0)),
            ],
            out_specs=pl.BlockSpec(memory_space=pl.ANY),
            scratch_shapes=[pltpu.VMEM((F32_SUBLANES, d), F32), pltpu.SemaphoreType.DMA(())],
        ),
        compiler_params=_params(("arbitrary",), 16),
        name="dispatch",
    )(tails, pends, dest3, x2d)


def _moe_kernel(be_ref, nu_ref, xs_ref, g_ref, wg_ref, wu_ref, wd_ref, ys_ref):
    @pl.when(pl.program_id(0) < nu_ref[0])
    def _():
        x = (_rms_scale(xs_ref[...]) * g_ref[...]).astype(BF16)
        a = _dot(x, wg_ref[...])
        b = _dot(x, wu_ref[...])
        h = (a * _sigmoid(a) * b).astype(BF16)
        ys_ref[...] = _dot(h, wd_ref[...])


def _moe(xs, g_ffn, wg_bf, wu_bf, wd_bf, blk_e, n_used, n_blocks):
    d = xs.shape[1]
    de = wg_bf.shape[2]
    bm = MOE_BM
    used = lambda i, be, nu: (jnp.minimum(i, nu[0] - 1), 0)
    return pl.pallas_call(
        _moe_kernel,
        out_shape=jax.ShapeDtypeStruct((n_blocks * bm, d), F32),
        grid_spec=pltpu.PrefetchScalarGridSpec(
            num_scalar_prefetch=2,
            grid=(n_blocks,),
            in_specs=[
                pl.BlockSpec((bm, d), used),
                pl.BlockSpec((1, d), lambda i, be, nu: (0, 0)),
                pl.BlockSpec((None, d, de), lambda i, be, nu: (be[i], 0, 0)),
                pl.BlockSpec((None, d, de), lambda i, be, nu: (be[i], 0, 0)),
                pl.BlockSpec((None, de, d), lambda i, be, nu: (be[i], 0, 0)),
            ],
            out_specs=pl.BlockSpec((bm, d), used),
        ),
        compiler_params=_params(("arbitrary",), 48),
        name="moe",
    )(blk_e, n_used, xs, g_ffn, wg_bf, wu_bf, wd_bf)


def _combine_kernel(dest_ref, x_ref, route_ref, g_ref, ys_hbm, out_ref, ybuf, sem):
    tb = ROWMOVE_TB

    def row_copy(t, k):
        return pltpu.make_async_copy(ys_hbm.at[pl.ds(dest_ref[0, 0, TOP_K * t + k], 1)],
                                     ybuf.at[k, pl.ds(t, 1)], sem)

    def start(t, _):
        for k in range(TOP_K):
            row_copy(t, k).start()
        return 0

    def wait(t, _):
        for k in range(TOP_K):
            row_copy(t, k).wait()
        return 0

    lax.fori_loop(0, tb, start, 0)
    lax.fori_loop(0, tb, wait, 0)
    w1 = route_ref[:, 2:3]
    w2 = route_ref[:, 3:4]
    x3 = x_ref[...] + (w1 * ybuf[0] + w2 * ybuf[1])
    out_ref[...] = _rms_scale(x3) * g_ref[...]


def _combine(x2d, route, g_final, ys, dest3):
    t, d = x2d.shape
    tb = ROWMOVE_TB
    return pl.pallas_call(
        _combine_kernel,
        out_shape=jax.ShapeDtypeStruct((t, d), F32),
        grid=(t // tb,),
        in_specs=[
            pl.BlockSpec((1, 1, TOP_K * tb), lambda i: (i, 0, 0), memory_space=pltpu.SMEM),
            pl.BlockSpec((tb, d), lambda i: (i, 0)),
            pl.BlockSpec((tb, LANES), lambda i: (i, 0)),
            pl.BlockSpec((1, d), lambda i: (0, 0)),
            pl.BlockSpec(memory_space=pl.ANY),
        ],
        out_specs=pl.BlockSpec((tb, d), lambda i: (i, 0)),
        scratch_shapes=[pltpu.VMEM((TOP_K, tb, d), F32), pltpu.SemaphoreType.DMA(())],
        compiler_params=_params(("arbitrary",), 32),
        name="combine",
    )(dest3, x2d, route, g_final, ys)


def _layer(x, mem, g_mix, w_in, g_gm, w_spatial, b_spatial, w_branch_sb, w_branch_gm, w_out,
           g_cross, g_mem, w_xq, w_xkv, w_xo, g_ffn, w_rg, b_rg, w_re, b_re,
           w_e_gate, w_e_up, w_e_down, g_final):
    batch, seq, d = x.shape
    n_mem = mem.shape[1]
    t = batch * seq
    row = lambda v: v.reshape(1, -1)
    x2d = x.reshape(t, d)

    qkv, u, vn, gates = _inproj(x2d, row(g_mix), w_in.astype(BF16), row(g_gm))
    o_sb = _sb_attn(qkv, batch, seq)
    b_bcast = jnp.broadcast_to(b_spatial[:, :, None], b_spatial.shape + (GM_CHUNK,))
    merged = _merge(o_sb, u, vn, gates, w_spatial, b_bcast,
                    w_branch_sb.astype(BF16), w_branch_gm.astype(BF16))
    kv = _memkv(mem.reshape(batch * n_mem, d), row(g_mem), w_xkv.astype(BF16))
    x2 = _cross(x2d, merged, w_out.astype(BF16), row(g_cross), w_xq.astype(BF16), kv,
                w_xo.astype(BF16), seq, n_mem)

    pad = LANES - N_GROUPS - N_EXPERTS
    w_r = jnp.concatenate([w_rg, w_re, jnp.zeros((d, pad), F32)], axis=1)
    b_r = jnp.concatenate([b_rg, b_re, jnp.zeros((pad,), F32)]).reshape(1, LANES)
    w_r_hi = w_r.astype(BF16)
    w_r_lo = (w_r - w_r_hi.astype(F32)).astype(BF16)
    route = _router(x2, row(g_ffn), w_r_hi, w_r_lo, b_r)

    n = t * TOP_K
    e_flat = route[:, :TOP_K].astype(I32).reshape(n)
    rank2d, cnt = _rank(e_flat.reshape(n // LANES, LANES))
    counts = cnt[:, 0].astype(I32)
    padded = ((counts + MOE_BM - 1) // MOE_BM) * MOE_BM
    pends = jnp.cumsum(padded)
    pstarts = pends - padded
    dest = pstarts[e_flat] + rank2d.reshape(n)
    n_blocks = n // MOE_BM + N_EXPERTS
    blk_start = jnp.arange(n_blocks, dtype=I32) * MOE_BM
    blk_e = jnp.minimum(jnp.sum(pends[None, :] <= blk_start[:, None], axis=1),
                        N_EXPERTS - 1).astype(I32)
    n_used = (pends[-1:] // MOE_BM).astype(I32)
    dest3 = dest.reshape(t // ROWMOVE_TB, 1, TOP_K * ROWMOVE_TB)

    xs = _dispatch(x2, dest3, (pstarts + counts).astype(I32), pends.astype(I32),
                   n_blocks * MOE_BM)
    ys = _moe(xs, row(g_ffn), w_e_gate.astype(BF16), w_e_up.astype(BF16), w_e_down.astype(BF16),
              blk_e, n_used, n_blocks)
    out = _combine(x2, route, row(g_final), ys, dest3)
    return out.reshape(batch, seq, d)


def kernel(x, mem, g_mix, w_in, g_gm, w_spatial, b_spatial, w_branch_sb, w_branch_gm, w_out,
           g_cross, g_mem, w_xq, w_xkv, w_xo, g_ffn, w_rg, b_rg, w_re, b_re,
           w_e_gate, w_e_up, w_e_down, g_final):
    assert w_in.shape[0] == 1, "single layer"
    return _layer(x, mem, g_mix[0], w_in[0], g_gm[0], w_spatial[0], b_spatial[0],
                  w_branch_sb[0], w_branch_gm[0], w_out[0], g_cross[0], g_mem[0], w_xq[0],
                  w_xkv[0], w_xo[0], g_ffn[0], w_rg[0], b_rg[0], w_re[0], b_re[0],
                  w_e_gate[0], w_e_up[0], w_e_down[0], g_final)
```

```python
import functools
import math

import jax
import jax.numpy as jnp
from jax import lax
from jax.experimental import pallas as pl
from jax.experimental.pallas import tpu as pltpu

F32 = jnp.float32
BF16 = jnp.bfloat16
I32 = jnp.int32

EPS = 1e-6
LANES = 128
MIB = 1024 * 1024

SB_HEADS = 8
GM_GROUPS = 8
GM_CHUNK = 128
X_HEADS = 4
N_GROUPS = 4
EXPERTS_PER_GROUP = 8
N_EXPERTS = N_GROUPS * EXPERTS_PER_GROUP
TOP_K = 2

INPROJ_BM = 512
INPROJ_BN = 1024
ATTN_QB = 128
ATTN_CHAINS = 4
MERGE_BM = 256
CROSS_BM = 256
ROUTER_BM = 512
RANK_ROWS = 8
MOE_BM = 256
COMBINE_BM = 256
ATTN_SKIP_LOG = 88.0


def _dot(a, b):
    return jnp.dot(a, b, preferred_element_type=F32)


def _dot_nt(a, b):
    return lax.dot_general(a, b, (((1,), (1,)), ((), ())), preferred_element_type=F32)


def _rms_scale(x):
    return x * lax.rsqrt(jnp.mean(x * x, axis=-1, keepdims=True) + EPS)


def _gelu(x):
    c = math.sqrt(2.0 / math.pi)
    return 0.5 * x * (1.0 + jnp.tanh(c * (x + 0.044715 * (x * x * x))))


def _sigmoid(x):
    return 0.5 * (jnp.tanh(0.5 * x) + 1.0)


def _params(semantics, vmem_mib):
    return pltpu.CompilerParams(dimension_semantics=semantics,
                                vmem_limit_bytes=vmem_mib * MIB)


def _inproj_kernel(x_ref, g_ref, w_ref, ggm_ref, qkv_ref, u_ref, vn_ref, gates_ref, hn_ref):
    j = pl.program_id(1)

    @pl.when(j == 0)
    def _():
        hn_ref[...] = (_rms_scale(x_ref[...]) * g_ref[...]).astype(BF16)

    def acc():
        return _dot(hn_ref[...], w_ref[...])

    @pl.when(j < 3)
    def _():
        qkv_ref[...] = acc().astype(BF16)

    @pl.when(j == 3)
    def _():
        u_ref[...] = _gelu(acc()).astype(BF16)

    @pl.when(j == 4)
    def _():
        vn_ref[...] = (_rms_scale(_gelu(acc())) * ggm_ref[...]).astype(BF16)

    @pl.when(j >= 5)
    def _():
        gates_ref[...] = _sigmoid(acc()).astype(BF16)


def _inproj(x2d, g_mix, w_in_bf, g_gm):
    t, d = x2d.shape
    bm, bn = INPROJ_BM, INPROJ_BN
    ncol = w_in_bf.shape[1] // bn
    assert ncol == 9 and t % bm == 0
    out_shape = (
        jax.ShapeDtypeStruct((t, 3 * bn), BF16),
        jax.ShapeDtypeStruct((t, bn), BF16),
        jax.ShapeDtypeStruct((t, bn), BF16),
        jax.ShapeDtypeStruct((t, 4 * bn), BF16),
    )
    return pl.pallas_call(
        _inproj_kernel,
        out_shape=out_shape,
        grid=(t // bm, ncol),
        in_specs=[
            pl.BlockSpec((bm, d), lambda i, j: (i, 0)),
            pl.BlockSpec((1, d), lambda i, j: (0, 0)),
            pl.BlockSpec((d, bn), lambda i, j: (0, j)),
            pl.BlockSpec((1, bn), lambda i, j: (0, 0)),
        ],
        out_specs=(
            pl.BlockSpec((bm, bn), lambda i, j: (i, jnp.minimum(j, 2))),
            pl.BlockSpec((bm, bn), lambda i, j: (i, 0)),
            pl.BlockSpec((bm, bn), lambda i, j: (i, 0)),
            pl.BlockSpec((bm, bn), lambda i, j: (i, jnp.clip(j - 5, 0, 3))),
        ),
        scratch_shapes=[pltpu.VMEM((bm, d), BF16)],
        compiler_params=_params(("parallel", "arbitrary"), 48),
        name="inproj",
    )(x2d, g_mix, w_in_bf, g_gm)


def _sb_attn_kernel(q_ref, k_ref, v_ref, o_ref, *, scale):
    s, _ = q_ref.shape
    qb = ATTN_QB
    row = lax.broadcasted_iota(I32, (qb, qb), 0)
    col = lax.broadcasted_iota(I32, (qb, qb), 1)
    strict = col < row
    above = (row > col).astype(BF16)
    above2 = jnp.concatenate([above, above], axis=0)

    def chunk(q, j, carry, diagonal):
        start = pl.multiple_of(j * qb, qb)
        kc = k_ref[pl.ds(start, qb), :]
        vc = v_ref[pl.ds(start, qb), :]
        z = _dot_nt(q, kc) * scale
        lp = jnp.log1p(jnp.exp(-jnp.abs(z)))
        log_sig = jnp.minimum(z, 0.0) - lp
        log_not = -jnp.maximum(z, 0.0) - lp
        if diagonal:
            log_not = jnp.where(strict, log_not, 0.0)
        hi = log_not.astype(BF16)
        mid = (log_not - hi.astype(F32)).astype(BF16)
        between = _dot(jnp.concatenate([hi, mid], axis=1), above2)
        a = jnp.exp(log_sig + between + carry)
        if diagonal:
            a = jnp.where(strict, a, 0.0)
        o = _dot(a.astype(BF16), vc)
        return o, carry + between[:, 0:1] + log_not[:, 0:1]

    chains = ATTN_CHAINS
    per_chain = s // qb // chains

    def qgroup(g, _):
        blocks = [c * per_chain + g for c in range(chains)]
        qs = [q_ref[pl.ds(pl.multiple_of(i * qb, qb), qb), :] for i in blocks]
        first = [chunk(q, i, jnp.zeros((qb, 1), F32), True) for q, i in zip(qs, blocks)]

        def pending(step, carries):
            need = [jnp.where(i - step >= 0, jnp.max(c), -jnp.inf) for i, c in zip(blocks, carries)]
            return functools.reduce(jnp.maximum, need)

        def cond(st):
            return st[3] > -ATTN_SKIP_LOG

        def body(st):
            step, outs, carries, _ = st
            new_outs, new_carries = [], []
            for q, i, o, carry in zip(qs, blocks, outs, carries):
                j = i - step
                o2, carry2 = chunk(q, jnp.maximum(j, 0), carry, False)
                new_outs.append(o + jnp.where(j >= 0, o2, 0.0))
                new_carries.append(carry2)
            return step + 1, tuple(new_outs), tuple(new_carries), pending(step + 1, new_carries)

        outs = tuple(o for o, _ in first)
        carries = tuple(c for _, c in first)
        _, outs, _, _ = lax.while_loop(cond, body, (1, outs, carries, pending(1, carries)))
        for i, o in zip(blocks, outs):
            o_ref[pl.ds(pl.multiple_of(i * qb, qb), qb), :] = o.astype(o_ref.dtype)
        return 0

    lax.fori_loop(0, per_chain, qgroup, 0)


def _sb_attn(qkv, batch, seq):
    width = qkv.shape[1] // 3
    dh = width // SB_HEADS
    assert dh == LANES and seq % (ATTN_QB * ATTN_CHAINS) == 0
    qkv3 = qkv.reshape(batch, seq, 3 * width)
    kern = functools.partial(_sb_attn_kernel, scale=1.0 / math.sqrt(dh))
    blk = (None, seq, dh)
    o = pl.pallas_call(
        kern,
        out_shape=jax.ShapeDtypeStruct((batch, seq, width), BF16),
        grid=(batch, SB_HEADS),
        in_specs=[
            pl.BlockSpec(blk, lambda b, h: (b, 0, h)),
            pl.BlockSpec(blk, lambda b, h: (b, 0, SB_HEADS + h)),
            pl.BlockSpec(blk, lambda b, h: (b, 0, 2 * SB_HEADS + h)),
        ],
        out_specs=pl.BlockSpec(blk, lambda b, h: (b, 0, h)),
        compiler_params=_params(("parallel", "parallel"), 48),
        name="sb_attn",
    )(qkv3, qkv3, qkv3)
    return o.reshape(batch * seq, width)


def _merge_kernel(osb_ref, u_ref, vn_ref, gsb_ref, ggm_ref, wsp_ref, bsp_ref, wsb_ref, wgm_ref,
                  out_ref, ogm_ref):
    bm = osb_ref.shape[0]
    c = GM_CHUNK
    row = lax.broadcasted_iota(I32, (c, c), 0)
    col = lax.broadcasted_iota(I32, (c, c), 1)
    causal = row >= col
    for g in range(GM_GROUPS):
        w = jnp.where(causal, wsp_ref[g], 0.0).astype(BF16)
        gs = slice(g * c, (g + 1) * c)
        for ch in range(bm // c):
            rs = slice(ch * c, (ch + 1) * c)
            mixed = _dot(w, vn_ref[rs, gs]) + bsp_ref[g]
            ogm_ref[rs, gs] = (u_ref[rs, gs].astype(F32) * mixed).astype(BF16)
    merged = (gsb_ref[...].astype(F32) * _dot(osb_ref[...], wsb_ref[...])
              + ggm_ref[...].astype(F32) * _dot(ogm_ref[...], wgm_ref[...]))
    out_ref[...] = merged.astype(BF16)


def _merge(o_sb, u, vn, gates, w_spatial, b_bcast, w_sb_bf, w_gm_bf):
    t, width = o_sb.shape
    d = w_sb_bf.shape[1]
    bm = MERGE_BM
    assert t % bm == 0 and bm % GM_CHUNK == 0 and width == GM_GROUPS * GM_CHUNK
    const3 = lambda i: (0, 0, 0)
    const2 = lambda i: (0, 0)
    return pl.pallas_call(
        _merge_kernel,
        out_shape=jax.ShapeDtypeStruct((t, d), BF16),
        grid=(t // bm,),
        in_specs=[
            pl.BlockSpec((bm, width), lambda i: (i, 0)),
            pl.BlockSpec((bm, width), lambda i: (i, 0)),
            pl.BlockSpec((bm, width), lambda i: (i, 0)),
            pl.BlockSpec((bm, d), lambda i: (i, 0)),
            pl.BlockSpec((bm, d), lambda i: (i, 1)),
            pl.BlockSpec(w_spatial.shape, const3),
            pl.BlockSpec(b_bcast.shape, const3),
            pl.BlockSpec(w_sb_bf.shape, const2),
            pl.BlockSpec(w_gm_bf.shape, const2),
        ],
        out_specs=pl.BlockSpec((bm, d), lambda i: (i, 0)),
        scratch_shapes=[pltpu.VMEM((bm, width), BF16)],
        compiler_params=_params(("parallel",), 48),
        name="merge",
    )(o_sb, u, vn, gates, gates, w_spatial, b_bcast, w_sb_bf, w_gm_bf)


def _memkv_kernel(mem_ref, g_ref, w_ref, kv_ref):
    mn = (_rms_scale(mem_ref[...]) * g_ref[...]).astype(BF16)
    kv_ref[...] = _dot(mn, w_ref[...]).astype(BF16)


def _memkv(mem2d, g_mem, w_xkv_bf):
    rows, d = mem2d.shape
    n = w_xkv_bf.shape[1]
    bm = 256
    assert rows % bm == 0
    return pl.pallas_call(
        _memkv_kernel,
        out_shape=jax.ShapeDtypeStruct((rows, n), BF16),
        grid=(rows // bm,),
        in_specs=[
            pl.BlockSpec((bm, d), lambda i: (i, 0)),
            pl.BlockSpec((1, d), lambda i: (0, 0)),
            pl.BlockSpec((d, n), lambda i: (0, 0)),
        ],
        out_specs=pl.BlockSpec((bm, n), lambda i: (i, 0)),
        compiler_params=_params(("parallel",), 32),
        name="memkv",
    )(mem2d, g_mem, w_xkv_bf)


def _cross_kernel(x_ref, m_ref, wout_ref, gc_ref, wxq_ref, kv_ref, wxo_ref, out_ref):
    xw = wxq_ref.shape[1]
    dh = xw // X_HEADS
    x1 = x_ref[...] + _dot(m_ref[...], wout_ref[...])
    hc = (_rms_scale(x1) * gc_ref[...]).astype(BF16)
    q = _dot(hc, wxq_ref[...]).astype(BF16)
    inv = 1.0 / math.sqrt(dh)
    heads = []
    for h in range(X_HEADS):
        kh = kv_ref[:, h * dh:(h + 1) * dh]
        vh = kv_ref[:, xw + h * dh: xw + (h + 1) * dh]
        s = _dot_nt(q[:, h * dh:(h + 1) * dh], kh) * inv
        e = jnp.exp(s - jnp.max(s, axis=-1, keepdims=True))
        p = e / jnp.sum(e, axis=-1, keepdims=True)
        heads.append(_dot(p.astype(BF16), vh).astype(BF16))
    o = jnp.concatenate(heads, axis=-1)
    out_ref[...] = x1 + _dot(o, wxo_ref[...])


def _cross(x2d, merged, w_out_bf, g_cross, w_xq_bf, kv, w_xo_bf, seq, n_mem):
    t, d = x2d.shape
    bm = CROSS_BM
    assert seq % bm == 0
    steps_per_batch = seq // bm
    const2 = lambda i: (0, 0)
    return pl.pallas_call(
        _cross_kernel,
        out_shape=jax.ShapeDtypeStruct((t, d), F32),
        grid=(t // bm,),
        in_specs=[
            pl.BlockSpec((bm, d), lambda i: (i, 0)),
            pl.BlockSpec((bm, d), lambda i: (i, 0)),
            pl.BlockSpec(w_out_bf.shape, const2),
            pl.BlockSpec((1, d), const2),
            pl.BlockSpec(w_xq_bf.shape, const2),
            pl.BlockSpec((n_mem, kv.shape[1]), lambda i: (i // steps_per_batch, 0)),
            pl.BlockSpec(w_xo_bf.shape, const2),
        ],
        out_specs=pl.BlockSpec((bm, d), lambda i: (i, 0)),
        compiler_params=_params(("parallel",), 48),
        name="cross",
    )(x2d, merged, w_out_bf, g_cross, w_xq_bf, kv, w_xo_bf)


def _router_kernel(x_ref, g_ref, whi_ref, wlo_ref, b_ref, route_ref):
    h = _rms_scale(x_ref[...]) * g_ref[...]
    h_hi = h.astype(BF16)
    h_lo = (h - h_hi.astype(F32)).astype(BF16)
    whi = whi_ref[...]
    lg = _dot(h_hi, whi) + _dot(h_hi, wlo_ref[...]) + _dot(h_lo, whi) + b_ref[...]
    lane = lax.broadcasted_iota(I32, lg.shape, 1)
    neg = -jnp.inf

    def first_max(mask):
        v = jnp.max(jnp.where(mask, lg, neg), axis=-1, keepdims=True)
        hit = jnp.logical_and(mask, lg == v)
        return v, jnp.min(jnp.where(hit, lane, LANES), axis=-1, keepdims=True)

    gmask = lane < N_GROUPS
    gmax, gidx = first_max(gmask)
    p_g = 1.0 / jnp.sum(jnp.where(gmask, jnp.exp(lg - gmax), 0.0), axis=-1, keepdims=True)
    lo = N_GROUPS + EXPERTS_PER_GROUP * gidx
    emask = jnp.logical_and(lane >= lo, lane < lo + EXPERTS_PER_GROUP)
    v1, i1 = first_max(emask)
    v2, i2 = first_max(jnp.logical_and(emask, lane != i1))
    t = jnp.exp(v2 - v1)
    w1 = p_g / (1.0 + t)
    w2 = w1 * t
    e1 = (i1 - N_GROUPS).astype(F32)
    e2 = (i2 - N_GROUPS).astype(F32)
    route_ref[...] = jnp.where(lane == 0, e1,
                               jnp.where(lane == 1, e2,
                                         jnp.where(lane == 2, w1,
                                                   jnp.where(lane == 3, w2, 0.0))))


def _router(x2d, g_ffn, w_hi, w_lo, b_r):
    t, d = x2d.shape
    bm = ROUTER_BM
    const2 = lambda i: (0, 0)
    return pl.pallas_call(
        _router_kernel,
        out_shape=jax.ShapeDtypeStruct((t, LANES), F32),
        grid=(t // bm,),
        in_specs=[
            pl.BlockSpec((bm, d), lambda i: (i, 0)),
            pl.BlockSpec((1, d), const2),
            pl.BlockSpec((d, LANES), const2),
            pl.BlockSpec((d, LANES), const2),
            pl.BlockSpec((1, LANES), const2),
        ],
        out_specs=pl.BlockSpec((bm, LANES), lambda i: (i, 0)),
        compiler_params=_params(("parallel",), 32),
        name="router",
    )(x2d, g_ffn, w_hi, w_lo, b_r)


def _rank_kernel(e_ref, rank_ref, cnt_ref, carry_ref):
    @pl.when(pl.program_id(0) == 0)
    def _():
        carry_ref[...] = jnp.zeros_like(carry_ref)

    ex = lax.broadcasted_iota(I32, (N_EXPERTS, LANES), 0)
    r = lax.broadcasted_iota(I32, (LANES, LANES), 0)
    c = lax.broadcasted_iota(I32, (LANES, LANES), 1)
    before = (r < c).astype(BF16)
    carry = carry_ref[...]
    for i in range(e_ref.shape[0]):
        hit = ex == e_ref[i:i + 1, :]
        onehot = jnp.where(hit, 1.0, 0.0)
        prefix = _dot(onehot.astype(BF16), before) + carry
        rank_ref[i:i + 1, :] = jnp.sum(jnp.where(hit, prefix, 0.0), axis=0, keepdims=True).astype(I32)
        carry = carry + jnp.sum(onehot, axis=1, keepdims=True)
    carry_ref[...] = carry
    cnt_ref[...] = carry


def _rank(e2d):
    rows = e2d.shape[0]
    rr = RANK_ROWS
    assert rows % rr == 0
    return pl.pallas_call(
        _rank_kernel,
        out_shape=(jax.ShapeDtypeStruct((rows, LANES), I32),
                   jax.ShapeDtypeStruct((N_EXPERTS, LANES), F32)),
        grid=(rows // rr,),
        in_specs=[pl.BlockSpec((rr, LANES), lambda i: (i, 0))],
        out_specs=(pl.BlockSpec((rr, LANES), lambda i: (i, 0)),
                   pl.BlockSpec((N_EXPERTS, LANES), lambda i: (0, 0))),
        scratch_shapes=[pltpu.VMEM((N_EXPERTS, LANES), F32)],
        compiler_params=_params(("arbitrary",), 16),
        name="rank",
    )(e2d)


def _moe_kernel(be_ref, nu_ref, src0_ref, src_next_ref, dst_prev_ref, x_hbm, g_ref,
                wg_ref, wu_ref, wd_ref, y_hbm, xbuf, ybuf, xn_ref, gsem, ssem):
    i = pl.program_id(0)
    n_used = nu_ref[0]
    bm = MOE_BM
    slot = lax.rem(i, 2)
    other = 1 - slot

    def gather(idx_ref, r, s):
        return pltpu.make_async_copy(x_hbm.at[pl.ds(idx_ref[0, 0, r], 1)],
                                     xbuf.at[s, pl.ds(r, 1)], gsem.at[s])

    def scatter(r, s):
        return pltpu.make_async_copy(ybuf.at[s, pl.ds(r, 1)],
                                     y_hbm.at[pl.ds(dst_prev_ref[0, 0, r], 1)], ssem.at[s])

    def wait_rows(make):
        for _ in range(bm):
            make().wait()

    @pl.when(i == 0)
    def _():
        ybuf[...] = jnp.zeros_like(ybuf)
        for r in range(bm):
            gather(src0_ref, r, 0).start()

    @pl.when(i <= n_used)
    def _():
        wait_rows(lambda: gather(src0_ref, 0, slot))

    @pl.when(i < n_used)
    def _():
        xn_ref[...] = (_rms_scale(xbuf[slot]) * g_ref[...]).astype(BF16)
        for r in range(bm):
            gather(src_next_ref, r, other).start()
            scatter(r, other).start()
        xn = xn_ref[...]
        a = _dot(xn, wg_ref[...])
        b = _dot(xn, wu_ref[...])
        h = (a * _sigmoid(a) * b).astype(BF16)
        ybuf[slot] = _dot(h, wd_ref[...])
        wait_rows(lambda: scatter(0, other))

    @pl.when(i == n_used)
    def _():
        for r in range(bm):
            scatter(r, other).start()
        wait_rows(lambda: scatter(0, other))


def _moe(x2d, g_ffn, wg_bf, wu_bf, wd_bf, blk_e, n_used, src3, dst3, n_blocks, n_out_rows):
    d = x2d.shape[1]
    de = wg_bf.shape[2]
    bm = MOE_BM
    last = lambda nu: nu[0] - 1
    smem_blk = lambda imap: pl.BlockSpec((1, 1, bm), imap, memory_space=pltpu.SMEM)
    expert = lambda i, be, nu: (be[jnp.minimum(i, last(nu))], 0, 0)
    return pl.pallas_call(
        _moe_kernel,
        out_shape=jax.ShapeDtypeStruct((n_out_rows, d), F32),
        grid_spec=pltpu.PrefetchScalarGridSpec(
            num_scalar_prefetch=2,
            grid=(n_blocks + 1,),
            in_specs=[
                smem_blk(lambda i, be, nu: (0, 0, 0)),
                smem_blk(lambda i, be, nu: (jnp.minimum(i + 1, last(nu)), 0, 0)),
                smem_blk(lambda i, be, nu: (jnp.minimum(i, nu[0]), 0, 0)),
                pl.BlockSpec(memory_space=pl.ANY),
                pl.BlockSpec((1, d), lambda i, be, nu: (0, 0)),
                pl.BlockSpec((None, d, de), expert),
                pl.BlockSpec((None, d, de), expert),
                pl.BlockSpec((None, de, d), expert),
            ],
            out_specs=pl.BlockSpec(memory_space=pl.ANY),
            scratch_shapes=[
                pltpu.VMEM((2, bm, d), F32),
                pltpu.VMEM((2, bm, d), F32),
                pltpu.VMEM((bm, d), BF16),
                pltpu.SemaphoreType.DMA((2,)),
                pltpu.SemaphoreType.DMA((2,)),
            ],
        ),
        compiler_params=_params(("arbitrary",), 56),
        name="moe",
    )(blk_e, n_used, src3, src3, dst3, x2d, g_ffn, wg_bf, wu_bf, wd_bf)


def _combine_kernel(x_ref, route_ref, g_ref, y_ref, out_ref):
    d = x_ref.shape[1]
    w1 = route_ref[:, 2:3]
    w2 = route_ref[:, 3:4]
    x3 = x_ref[...] + (w1 * y_ref[:, :d] + w2 * y_ref[:, d:])
    out_ref[...] = _rms_scale(x3) * g_ref[...]


def _combine(x2d, route, g_final, y_pairs):
    t, d = x2d.shape
    tb = COMBINE_BM
    return pl.pallas_call(
        _combine_kernel,
        out_shape=jax.ShapeDtypeStruct((t, d), F32),
        grid=(t // tb,),
        in_specs=[
            pl.BlockSpec((tb, d), lambda i: (i, 0)),
            pl.BlockSpec((tb, LANES), lambda i: (i, 0)),
            pl.BlockSpec((1, d), lambda i: (0, 0)),
            pl.BlockSpec((tb, TOP_K * d), lambda i: (i, 0)),
        ],
        out_specs=pl.BlockSpec((tb, d), lambda i: (i, 0)),
        compiler_params=_params(("parallel",), 40),
        name="combine",
    )(x2d, route, g_final, y_pairs)


def _layer(x, mem, g_mix, w_in, g_gm, w_spatial, b_spatial, w_branch_sb, w_branch_gm, w_out,
           g_cross, g_mem, w_xq, w_xkv, w_xo, g_ffn, w_rg, b_rg, w_re, b_re,
           w_e_gate, w_e_up, w_e_down, g_final):
    batch, seq, d = x.shape
    n_mem = mem.shape[1]
    t = batch * seq
    row = lambda v: v.reshape(1, -1)
    x2d = x.reshape(t, d)

    qkv, u, vn, gates = _inproj(x2d, row(g_mix), w_in.astype(BF16), row(g_gm))
    o_sb = _sb_attn(qkv, batch, seq)
    b_bcast = jnp.broadcast_to(b_spatial[:, :, None], b_spatial.shape + (GM_CHUNK,))
    merged = _merge(o_sb, u, vn, gates, w_spatial, b_bcast,
                    w_branch_sb.astype(BF16), w_branch_gm.astype(BF16))
    kv = _memkv(mem.reshape(batch * n_mem, d), row(g_mem), w_xkv.astype(BF16))
    x2 = _cross(x2d, merged, w_out.astype(BF16), row(g_cross), w_xq.astype(BF16), kv,
                w_xo.astype(BF16), seq, n_mem)

    pad = LANES - N_GROUPS - N_EXPERTS
    w_r = jnp.concatenate([w_rg, w_re, jnp.zeros((d, pad), F32)], axis=1)
    b_r = jnp.concatenate([b_rg, b_re, jnp.zeros((pad,), F32)]).reshape(1, LANES)
    w_r_hi = w_r.astype(BF16)
    w_r_lo = (w_r - w_r_hi.astype(F32)).astype(BF16)
    route = _router(x2, row(g_ffn), w_r_hi, w_r_lo, b_r)

    n = t * TOP_K
    e_flat = route[:, :TOP_K].astype(I32).reshape(n)
    rank2d, cnt = _rank(e_flat.reshape(n // LANES, LANES))
    counts = cnt[:, 0].astype(I32)
    padded = ((counts + MOE_BM - 1) // MOE_BM) * MOE_BM
    pends = jnp.cumsum(padded)
    pstarts = pends - padded
    dest = pstarts[e_flat] + rank2d.reshape(n)
    n_blocks = n // MOE_BM + N_EXPERTS
    blk_start = jnp.arange(n_blocks, dtype=I32) * MOE_BM
    blk_e = jnp.minimum(jnp.sum(pends[None, :] <= blk_start[:, None], axis=1),
                        N_EXPERTS - 1).astype(I32)
    n_used = (pends[-1:] // MOE_BM).astype(I32)
    inv = jnp.full((n_blocks * MOE_BM,), -1, I32).at[dest].set(jnp.arange(n, dtype=I32))
    inv = inv.reshape(n_blocks, MOE_BM)
    src3 = (jnp.maximum(inv, 0) // TOP_K).reshape(n_blocks, 1, MOE_BM)
    spare = n + (jnp.arange(n_blocks, dtype=I32)[:, None] % 2) * MOE_BM + jnp.arange(MOE_BM, dtype=I32)
    dst = jnp.where(inv >= 0, inv, spare)
    first = (n + MOE_BM + jnp.arange(MOE_BM, dtype=I32))[None]
    dst3 = jnp.concatenate([first, dst], axis=0).reshape(n_blocks + 1, 1, MOE_BM)

    y = _moe(x2, row(g_ffn), w_e_gate.astype(BF16), w_e_up.astype(BF16), w_e_down.astype(BF16),
             blk_e, n_used, src3, dst3, n_blocks, n + 2 * MOE_BM)
    out = _combine(x2, route, row(g_final), y.reshape(t + MOE_BM, TOP_K * d))
    return out.reshape(batch, seq, d)


def kernel(x, mem, g_mix, w_in, g_gm, w_spatial, b_spatial, w_branch_sb, w_branch_gm, w_out,
           g_cross, g_mem, w_xq, w_xkv, w_xo, g_ffn, w_rg, b_rg, w_re, b_re,
           w_e_gate, w_e_up, w_e_down, g_final):
    assert w_in.shape[0] == 1, "single layer"
    return _layer(x, mem, g_mix[0], w_in[0], g_gm[0], w_spatial[0], b_spatial[0],
                  w_branch_sb[0], w_branch_gm[0], w_out[0], g_cross[0], g_mem[0], w_xq[0],
                  w_xkv[0], w_xo[0], g_ffn[0], w_rg[0], b_rg[0], w_re[0], b_re[0],
                  w_e_gate[0], w_e_up[0], w_e_down[0], g_final)
```

```python
import functools
import math

import jax
import jax.numpy as jnp
from jax import lax
from jax.experimental import pallas as pl
from jax.experimental.pallas import tpu as pltpu

F32 = jnp.float32
BF16 = jnp.bfloat16
I32 = jnp.int32

EPS = 1e-6
LANES = 128
MIB = 1024 * 1024

SB_HEADS = 8
GM_GROUPS = 8
GM_CHUNK = 128
X_HEADS = 4
N_GROUPS = 4
EXPERTS_PER_GROUP = 8
N_EXPERTS = N_GROUPS * EXPERTS_PER_GROUP
TOP_K = 2

INPROJ_BM = 512
INPROJ_BN = 1024
ATTN_QB = 128
ATTN_CHAINS = 4
MERGE_BM = 256
CROSS_BM = 256
ROUTER_BM = 512
RANK_ROWS = 8
MOE_BM = 256
COMBINE_BM = 256
ATTN_SKIP_LOG = 88.0


def _dot(a, b):
    return jnp.dot(a, b, preferred_element_type=F32)


def _dot_nt(a, b):
    return lax.dot_general(a, b, (((1,), (1,)), ((), ())), preferred_element_type=F32)


def _rms_scale(x):
    return x * lax.rsqrt(jnp.mean(x * x, axis=-1, keepdims=True) + EPS)


def _gelu(x):
    c = math.sqrt(2.0 / math.pi)
    return 0.5 * x * (1.0 + jnp.tanh(c * (x + 0.044715 * (x * x * x))))


def _sigmoid(x):
    return 0.5 * (jnp.tanh(0.5 * x) + 1.0)


def _params(semantics, vmem_mib):
    return pltpu.CompilerParams(dimension_semantics=semantics,
                                vmem_limit_bytes=vmem_mib * MIB)


def _inproj_kernel(x_ref, g_ref, w_ref, ggm_ref, qkv_ref, u_ref, vn_ref, gates_ref, hn_ref):
    j = pl.program_id(1)

    @pl.when(j == 0)
    def _():
        hn_ref[...] = (_rms_scale(x_ref[...]) * g_ref[...]).astype(BF16)

    def acc():
        return _dot(hn_ref[...], w_ref[...])

    @pl.when(j < 3)
    def _():
        qkv_ref[...] = acc().astype(BF16)

    @pl.when(j == 3)
    def _():
        u_ref[...] = _gelu(acc()).astype(BF16)

    @pl.when(j == 4)
    def _():
        vn_ref[...] = (_rms_scale(_gelu(acc())) * ggm_ref[...]).astype(BF16)

    @pl.when(j >= 5)
    def _():
        gates_ref[...] = _sigmoid(acc()).astype(BF16)


def _inproj(x2d, g_mix, w_in_bf, g_gm):
    t, d = x2d.shape
    bm, bn = INPROJ_BM, INPROJ_BN
    ncol = w_in_bf.shape[1] // bn
    assert ncol == 9 and t % bm == 0
    out_shape = (
        jax.ShapeDtypeStruct((t, 3 * bn), BF16),
        jax.ShapeDtypeStruct((t, bn), BF16),
        jax.ShapeDtypeStruct((t, bn), BF16),
        jax.ShapeDtypeStruct((t, 4 * bn), BF16),
    )
    return pl.pallas_call(
        _inproj_kernel,
        out_shape=out_shape,
        grid=(t // bm, ncol),
        in_specs=[
            pl.BlockSpec((bm, d), lambda i, j: (i, 0)),
            pl.BlockSpec((1, d), lambda i, j: (0, 0)),
            pl.BlockSpec((d, bn), lambda i, j: (0, j)),
            pl.BlockSpec((1, bn), lambda i, j: (0, 0)),
        ],
        out_specs=(
            pl.BlockSpec((bm, bn), lambda i, j: (i, jnp.minimum(j, 2))),
            pl.BlockSpec((bm, bn), lambda i, j: (i, 0)),
            pl.BlockSpec((bm, bn), lambda i, j: (i, 0)),
            pl.BlockSpec((bm, bn), lambda i, j: (i, jnp.clip(j - 5, 0, 3))),
        ),
        scratch_shapes=[pltpu.VMEM((bm, d), BF16)],
        compiler_params=_params(("parallel", "arbitrary"), 48),
        name="inproj",
    )(x2d, g_mix, w_in_bf, g_gm)


def _sb_attn_kernel(q_ref, k_ref, v_ref, o_ref, *, scale):
    s, _ = q_ref.shape
    qb = ATTN_QB
    row = lax.broadcasted_iota(I32, (qb, qb), 0)
    col = lax.broadcasted_iota(I32, (qb, qb), 1)
    strict = col < row
    above = (row > col).astype(BF16)
    above2 = jnp.concatenate([above, above], axis=0)

    def chunk(q, j, carry, diagonal):
        start = pl.multiple_of(j * qb, qb)
        kc = k_ref[pl.ds(start, qb), :]
        vc = v_ref[pl.ds(start, qb), :]
        z = _dot_nt(q, kc) * scale
        lp = jnp.log1p(jnp.exp(-jnp.abs(z)))
        log_sig = jnp.minimum(z, 0.0) - lp
        log_not = -jnp.maximum(z, 0.0) - lp
        if diagonal:
            log_not = jnp.where(strict, log_not, 0.0)
        hi = log_not.astype(BF16)
        mid = (log_not - hi.astype(F32)).astype(BF16)
        between = _dot(jnp.concatenate([hi, mid], axis=1), above2)
        a = jnp.exp(log_sig + between + carry)
        if diagonal:
            a = jnp.where(strict, a, 0.0)
        o = _dot(a.astype(BF16), vc)
        return o, carry + between[:, 0:1] + log_not[:, 0:1]

    chains = ATTN_CHAINS
    per_chain = s // qb // chains

    def qgroup(g, _):
        blocks = [c * per_chain + g for c in range(chains)]
        qs = [q_ref[pl.ds(pl.multiple_of(i * qb, qb), qb), :] for i in blocks]
        first = [chunk(q, i, jnp.zeros((qb, 1), F32), True) for q, i in zip(qs, blocks)]

        def pending(step, carries):
            need = [jnp.where(i - step >= 0, jnp.max(c), -jnp.inf) for i, c in zip(blocks, carries)]
            return functools.reduce(jnp.maximum, need)

        def cond(st):
            return st[3] > -ATTN_SKIP_LOG

        def body(st):
            step, outs, carries, _ = st
            new_outs, new_carries = [], []
            for q, i, o, carry in zip(qs, blocks, outs, carries):
                j = i - step
                o2, carry2 = chunk(q, jnp.maximum(j, 0), carry, False)
                new_outs.append(o + jnp.where(j >= 0, o2, 0.0))
                new_carries.append(carry2)
            return step + 1, tuple(new_outs), tuple(new_carries), pending(step + 1, new_carries)

        outs = tuple(o for o, _ in first)
        carries = tuple(c for _, c in first)
        _, outs, _, _ = lax.while_loop(cond, body, (1, outs, carries, pending(1, carries)))
        for i, o in zip(blocks, outs):
            o_ref[pl.ds(pl.multiple_of(i * qb, qb), qb), :] = o.astype(o_ref.dtype)
        return 0

    lax.fori_loop(0, per_chain, qgroup, 0)


def _sb_attn(qkv, batch, seq):
    width = qkv.shape[1] // 3
    dh = width // SB_HEADS
    assert dh == LANES and seq % (ATTN_QB * ATTN_CHAINS) == 0
    qkv3 = qkv.reshape(batch, seq, 3 * width)
    kern = functools.partial(_sb_attn_kernel, scale=1.0 / math.sqrt(dh))
    blk = (None, seq, dh)
    o = pl.pallas_call(
        kern,
        out_shape=jax.ShapeDtypeStruct((batch, seq, width), BF16),
        grid=(batch, SB_HEADS),
        in_specs=[
            pl.BlockSpec(blk, lambda b, h: (b, 0, h)),
            pl.BlockSpec(blk, lambda b, h: (b, 0, SB_HEADS + h)),
            pl.BlockSpec(blk, lambda b, h: (b, 0, 2 * SB_HEADS + h)),
        ],
        out_specs=pl.BlockSpec(blk, lambda b, h: (b, 0, h)),
        compiler_params=_params(("parallel", "parallel"), 48),
        name="sb_attn",
    )(qkv3, qkv3, qkv3)
    return o.reshape(batch * seq, width)


def _merge_kernel(osb_ref, u_ref, vn_ref, gsb_ref, ggm_ref, wsp_ref, bsp_ref, wsb_ref, wgm_ref,
                  out_ref, ogm_ref):
    bm = osb_ref.shape[0]
    c = GM_CHUNK
    row = lax.broadcasted_iota(I32, (c, c), 0)
    col = lax.broadcasted_iota(I32, (c, c), 1)
    causal = row >= col
    for g in range(GM_GROUPS):
        w = jnp.where(causal, wsp_ref[g], 0.0).astype(BF16)
        gs = slice(g * c, (g + 1) * c)
        for ch in range(bm // c):
            rs = slice(ch * c, (ch + 1) * c)
            mixed = _dot(w, vn_ref[rs, gs]) + bsp_ref[g]
            ogm_ref[rs, gs] = (u_ref[rs, gs].astype(F32) * mixed).astype(BF16)
    merged = (gsb_ref[...].astype(F32) * _dot(osb_ref[...], wsb_ref[...])
              + ggm_ref[...].astype(F32) * _dot(ogm_ref[...], wgm_ref[...]))
    out_ref[...] = merged.astype(BF16)


def _merge(o_sb, u, vn, gates, w_spatial, b_bcast, w_sb_bf, w_gm_bf):
    t, width = o_sb.shape
    d = w_sb_bf.shape[1]
    bm = MERGE_BM
    assert t % bm == 0 and bm % GM_CHUNK == 0 and width == GM_GROUPS * GM_CHUNK
    const3 = lambda i: (0, 0, 0)
    const2 = lambda i: (0, 0)
    return pl.pallas_call(
        _merge_kernel,
        out_shape=jax.ShapeDtypeStruct((t, d), BF16),
        grid=(t // bm,),
        in_specs=[
            pl.BlockSpec((bm, width), lambda i: (i, 0)),
            pl.BlockSpec((bm, width), lambda i: (i, 0)),
            pl.BlockSpec((bm, width), lambda i: (i, 0)),
            pl.BlockSpec((bm, d), lambda i: (i, 0)),
            pl.BlockSpec((bm, d), lambda i: (i, 1)),
            pl.BlockSpec(w_spatial.shape, const3),
            pl.BlockSpec(b_bcast.shape, const3),
            pl.BlockSpec(w_sb_bf.shape, const2),
            pl.BlockSpec(w_gm_bf.shape, const2),
        ],
        out_specs=pl.BlockSpec((bm, d), lambda i: (i, 0)),
        scratch_shapes=[pltpu.VMEM((bm, width), BF16)],
        compiler_params=_params(("parallel",), 48),
        name="merge",
    )(o_sb, u, vn, gates, gates, w_spatial, b_bcast, w_sb_bf, w_gm_bf)


def _memkv_kernel(mem_ref, g_ref, w_ref, kv_ref):
    mn = (_rms_scale(mem_ref[...]) * g_ref[...]).astype(BF16)
    kv_ref[...] = _dot(mn, w_ref[...]).astype(BF16)


def _memkv(mem2d, g_mem, w_xkv_bf):
    rows, d = mem2d.shape
    n = w_xkv_bf.shape[1]
    bm = 256
    assert rows % bm == 0
    return pl.pallas_call(
        _memkv_kernel,
        out_shape=jax.ShapeDtypeStruct((rows, n), BF16),
        grid=(rows // bm,),
        in_specs=[
            pl.BlockSpec((bm, d), lambda i: (i, 0)),
            pl.BlockSpec((1, d), lambda i: (0, 0)),
            pl.BlockSpec((d, n), lambda i: (0, 0)),
        ],
        out_specs=pl.BlockSpec((bm, n), lambda i: (i, 0)),
        compiler_params=_params(("parallel",), 32),
        name="memkv",
    )(mem2d, g_mem, w_xkv_bf)


def _cross_kernel(x_ref, m_ref, wout_ref, gc_ref, wxq_ref, kv_ref, wxo_ref, out_ref):
    xw = wxq_ref.shape[1]
    dh = xw // X_HEADS
    x1 = x_ref[...] + _dot(m_ref[...], wout_ref[...])
    hc = (_rms_scale(x1) * gc_ref[...]).astype(BF16)
    q = _dot(hc, wxq_ref[...]).astype(BF16)
    inv = 1.0 / math.sqrt(dh)
    heads = []
    for h in range(X_HEADS):
        kh = kv_ref[:, h * dh:(h + 1) * dh]
        vh = kv_ref[:, xw + h * dh: xw + (h + 1) * dh]
        s = _dot_nt(q[:, h * dh:(h + 1) * dh], kh) * inv
        e = jnp.exp(s - jnp.max(s, axis=-1, keepdims=True))
        p = e / jnp.sum(e, axis=-1, keepdims=True)
        heads.append(_dot(p.astype(BF16), vh).astype(BF16))
    o = jnp.concatenate(heads, axis=-1)
    out_ref[...] = x1 + _dot(o, wxo_ref[...])


def _cross(x2d, merged, w_out_bf, g_cross, w_xq_bf, kv, w_xo_bf, seq, n_mem):
    t, d = x2d.shape
    bm = CROSS_BM
    assert seq % bm == 0
    steps_per_batch = seq // bm
    const2 = lambda i: (0, 0)
    return pl.pallas_call(
        _cross_kernel,
        out_shape=jax.ShapeDtypeStruct((t, d), F32),
        grid=(t // bm,),
        in_specs=[
            pl.BlockSpec((bm, d), lambda i: (i, 0)),
            pl.BlockSpec((bm, d), lambda i: (i, 0)),
            pl.BlockSpec(w_out_bf.shape, const2),
            pl.BlockSpec((1, d), const2),
            pl.BlockSpec(w_xq_bf.shape, const2),
            pl.BlockSpec((n_mem, kv.shape[1]), lambda i: (i // steps_per_batch, 0)),
            pl.BlockSpec(w_xo_bf.shape, const2),
        ],
        out_specs=pl.BlockSpec((bm, d), lambda i: (i, 0)),
        compiler_params=_params(("parallel",), 48),
        name="cross",
    )(x2d, merged, w_out_bf, g_cross, w_xq_bf, kv, w_xo_bf)


def _router_kernel(x_ref, g_ref, whi_ref, wlo_ref, b_ref, route_ref):
    h = _rms_scale(x_ref[...]) * g_ref[...]
    h_hi = h.astype(BF16)
    h_lo = (h - h_hi.astype(F32)).astype(BF16)
    whi = whi_ref[...]
    lg = _dot(h_hi, whi) + _dot(h_hi, wlo_ref[...]) + _dot(h_lo, whi) + b_ref[...]
    lane = lax.broadcasted_iota(I32, lg.shape, 1)
    neg = -jnp.inf

    def first_max(mask):
        v = jnp.max(jnp.where(mask, lg, neg), axis=-1, keepdims=True)
        hit = jnp.logical_and(mask, lg == v)
        return v, jnp.min(jnp.where(hit, lane, LANES), axis=-1, keepdims=True)

    gmask = lane < N_GROUPS
    gmax, gidx = first_max(gmask)
    p_g = 1.0 / jnp.sum(jnp.where(gmask, jnp.exp(lg - gmax), 0.0), axis=-1, keepdims=True)
    lo = N_GROUPS + EXPERTS_PER_GROUP * gidx
    emask = jnp.logical_and(lane >= lo, lane < lo + EXPERTS_PER_GROUP)
    v1, i1 = first_max(emask)
    v2, i2 = first_max(jnp.logical_and(emask, lane != i1))
    t = jnp.exp(v2 - v1)
    w1 = p_g / (1.0 + t)
    w2 = w1 * t
    e1 = (i1 - N_GROUPS).astype(F32)
    e2 = (i2 - N_GROUPS).astype(F32)
    route_ref[...] = jnp.where(lane == 0, e1,
                               jnp.where(lane == 1, e2,
                                         jnp.where(lane == 2, w1,
                                                   jnp.where(lane == 3, w2, 0.0))))


def _router(x2d, g_ffn, w_hi, w_lo, b_r):
    t, d = x2d.shape
    bm = ROUTER_BM
    const2 = lambda i: (0, 0)
    return pl.pallas_call(
        _router_kernel,
        out_shape=jax.ShapeDtypeStruct((t, LANES), F32),
        grid=(t // bm,),
        in_specs=[
            pl.BlockSpec((bm, d), lambda i: (i, 0)),
            pl.BlockSpec((1, d), const2),
            pl.BlockSpec((d, LANES), const2),
            pl.BlockSpec((d, LANES), const2),
            pl.BlockSpec((1, LANES), const2),
        ],
        out_specs=pl.BlockSpec((bm, LANES), lambda i: (i, 0)),
        compiler_params=_params(("parallel",), 32),
        name="router",
    )(x2d, g_ffn, w_hi, w_lo, b_r)


def _rank_kernel(e_ref, rank_ref, cnt_ref, carry_ref):
    @pl.when(pl.program_id(0) == 0)
    def _():
        carry_ref[...] = jnp.zeros_like(carry_ref)

    ex = lax.broadcasted_iota(I32, (N_EXPERTS, LANES), 0)
    r = lax.broadcasted_iota(I32, (LANES, LANES), 0)
    c = lax.broadcasted_iota(I32, (LANES, LANES), 1)
    before = (r < c).astype(BF16)
    carry = carry_ref[...]
    for i in range(e_ref.shape[0]):
        hit = ex == e_ref[i:i + 1, :]
        onehot = jnp.where(hit, 1.0, 0.0)
        prefix = _dot(onehot.astype(BF16), before) + carry
        rank_ref[i:i + 1, :] = jnp.sum(jnp.where(hit, prefix, 0.0), axis=0, keepdims=True).astype(I32)
        carry = carry + jnp.sum(onehot, axis=1, keepdims=True)
    carry_ref[...] = carry
    cnt_ref[...] = carry


def _rank(e2d):
    rows = e2d.shape[0]
    rr = RANK_ROWS
    assert rows % rr == 0
    return pl.pallas_call(
        _rank_kernel,
        out_shape=(jax.ShapeDtypeStruct((rows, LANES), I32),
                   jax.ShapeDtypeStruct((N_EXPERTS, LANES), F32)),
        grid=(rows // rr,),
        in_specs=[pl.BlockSpec((rr, LANES), lambda i: (i, 0))],
        out_specs=(pl.BlockSpec((rr, LANES), lambda i: (i, 0)),
                   pl.BlockSpec((N_EXPERTS, LANES), lambda i: (0, 0))),
        scratch_shapes=[pltpu.VMEM((N_EXPERTS, LANES), F32)],
        compiler_params=_params(("arbitrary",), 16),
        name="rank",
    )(e2d)


def _moe_kernel(be_ref, nu_ref, src0_ref, src_next_ref, dst_prev_ref, x_hbm, g_ref,
                wg_ref, wu_ref, wd_ref, y_hbm, xbuf, ybuf, xn_ref, gsem, ssem):
    i = pl.program_id(0)
    n_used = nu_ref[0]
    bm = MOE_BM

    def gather(idx_ref, r, s):
        return pltpu.make_async_copy(x_hbm.at[pl.ds(idx_ref[0, 0, r], 1)],
                                     xbuf.at[s, pl.ds(r, 1)], gsem.at[s])

    def scatter(r, s):
        return pltpu.make_async_copy(ybuf.at[s, pl.ds(r, 1)],
                                     y_hbm.at[pl.ds(dst_prev_ref[0, 0, r], 1)], ssem.at[s])

    def wait_rows(make):
        for _ in range(bm):
            make().wait()

    def start_scatters(s):
        for r in range(bm):
            scatter(r, s).start(priority=r % 2)

    @pl.when(i == 0)
    def _():
        ybuf[...] = jnp.zeros_like(ybuf)
        for r in range(bm):
            gather(src0_ref, r, 0).start()

    def step(slot):
        other = 1 - slot

        @pl.when(i <= n_used)
        def _():
            wait_rows(lambda: gather(src0_ref, 0, slot))

        @pl.when(i < n_used)
        def _():
            xn_ref[...] = (_rms_scale(xbuf[slot]) * g_ref[...]).astype(BF16)
            for r in range(bm):
                gather(src_next_ref, r, other).start()
            start_scatters(other)
            xn = xn_ref[...]
            a = _dot(xn, wg_ref[...])
            b = _dot(xn, wu_ref[...])
            h = (a * _sigmoid(a) * b).astype(BF16)
            ybuf[slot] = _dot(h, wd_ref[...])
            wait_rows(lambda: scatter(0, other))

        @pl.when(i == n_used)
        def _():
            start_scatters(other)
            wait_rows(lambda: scatter(0, other))

    parity = lax.rem(i, 2)
    pl.when(parity == 0)(lambda: step(0))
    pl.when(parity == 1)(lambda: step(1))


def _moe(x2d, g_ffn, wg_bf, wu_bf, wd_bf, blk_e, n_used, src3, dst3, n_blocks, n_out_rows):
    d = x2d.shape[1]
    de = wg_bf.shape[2]
    bm = MOE_BM
    last = lambda nu: nu[0] - 1
    smem_blk = lambda imap: pl.BlockSpec((1, 1, bm), imap, memory_space=pltpu.SMEM)
    expert = lambda i, be, nu: (be[jnp.minimum(i, last(nu))], 0, 0)
    return pl.pallas_call(
        _moe_kernel,
        out_shape=jax.ShapeDtypeStruct((n_out_rows, d), F32),
        grid_spec=pltpu.PrefetchScalarGridSpec(
            num_scalar_prefetch=2,
            grid=(n_blocks + 1,),
            in_specs=[
                smem_blk(lambda i, be, nu: (0, 0, 0)),
                smem_blk(lambda i, be, nu: (jnp.minimum(i + 1, last(nu)), 0, 0)),
                smem_blk(lambda i, be, nu: (jnp.minimum(i, nu[0]), 0, 0)),
                pl.BlockSpec(memory_space=pl.ANY),
                pl.BlockSpec((1, d), lambda i, be, nu: (0, 0)),
                pl.BlockSpec((None, d, de), expert),
                pl.BlockSpec((None, d, de), expert),
                pl.BlockSpec((None, de, d), expert),
            ],
            out_specs=pl.BlockSpec(memory_space=pl.ANY),
            scratch_shapes=[
                pltpu.VMEM((2, bm, d), F32),
                pltpu.VMEM((2, bm, d), F32),
                pltpu.VMEM((bm, d), BF16),
                pltpu.SemaphoreType.DMA((2,)),
                pltpu.SemaphoreType.DMA((2,)),
            ],
        ),
        compiler_params=_params(("arbitrary",), 56),
        name="moe",
    )(blk_e, n_used, src3, src3, dst3, x2d, g_ffn, wg_bf, wu_bf, wd_bf)


def _combine_kernel(x_ref, route_ref, g_ref, y1_ref, y2_ref, out_ref):
    w1 = route_ref[:, 2:3]
    w2 = route_ref[:, 3:4]
    x3 = x_ref[...] + (w1 * y1_ref[...] + w2 * y2_ref[...])
    out_ref[...] = _rms_scale(x3) * g_ref[...]


def _combine(x2d, route, g_final, y):
    t, d = x2d.shape
    tb = COMBINE_BM
    steps = t // tb
    return pl.pallas_call(
        _combine_kernel,
        out_shape=jax.ShapeDtypeStruct((t, d), F32),
        grid=(steps,),
        in_specs=[
            pl.BlockSpec((tb, d), lambda i: (i, 0)),
            pl.BlockSpec((tb, LANES), lambda i: (i, 0)),
            pl.BlockSpec((1, d), lambda i: (0, 0)),
            pl.BlockSpec((tb, d), lambda i: (i, 0)),
            pl.BlockSpec((tb, d), lambda i: (steps + i, 0)),
        ],
        out_specs=pl.BlockSpec((tb, d), lambda i: (i, 0)),
        compiler_params=_params(("parallel",), 40),
        name="combine",
    )(x2d, route, g_final, y, y)


def _layer(x, mem, g_mix, w_in, g_gm, w_spatial, b_spatial, w_branch_sb, w_branch_gm, w_out,
           g_cross, g_mem, w_xq, w_xkv, w_xo, g_ffn, w_rg, b_rg, w_re, b_re,
           w_e_gate, w_e_up, w_e_down, g_final):
    batch, seq, d = x.shape
    n_mem = mem.shape[1]
    t = batch * seq
    row = lambda v: v.reshape(1, -1)
    x2d = x.reshape(t, d)

    qkv, u, vn, gates = _inproj(x2d, row(g_mix), w_in.astype(BF16), row(g_gm))
    o_sb = _sb_attn(qkv, batch, seq)
    b_bcast = jnp.broadcast_to(b_spatial[:, :, None], b_spatial.shape + (GM_CHUNK,))
    merged = _merge(o_sb, u, vn, gates, w_spatial, b_bcast,
                    w_branch_sb.astype(BF16), w_branch_gm.astype(BF16))
    kv = _memkv(mem.reshape(batch * n_mem, d), row(g_mem), w_xkv.astype(BF16))
    x2 = _cross(x2d, merged, w_out.astype(BF16), row(g_cross), w_xq.astype(BF16), kv,
                w_xo.astype(BF16), seq, n_mem)

    pad = LANES - N_GROUPS - N_EXPERTS
    w_r = jnp.concatenate([w_rg, w_re, jnp.zeros((d, pad), F32)], axis=1)
    b_r = jnp.concatenate([b_rg, b_re, jnp.zeros((pad,), F32)]).reshape(1, LANES)
    w_r_hi = w_r.astype(BF16)
    w_r_lo = (w_r - w_r_hi.astype(F32)).astype(BF16)
    route = _router(x2, row(g_ffn), w_r_hi, w_r_lo, b_r)

    n = t * TOP_K
    e_flat = route[:, :TOP_K].astype(I32).reshape(n)
    rank2d, cnt = _rank(e_flat.reshape(n // LANES, LANES))
    counts = cnt[:, 0].astype(I32)
    padded = ((counts + MOE_BM - 1) // MOE_BM) * MOE_BM
    pends = jnp.cumsum(padded)
    pstarts = pends - padded
    dest = pstarts[e_flat] + rank2d.reshape(n)
    n_blocks = n // MOE_BM + N_EXPERTS
    blk_start = jnp.arange(n_blocks, dtype=I32) * MOE_BM
    blk_e = jnp.minimum(jnp.sum(pends[None, :] <= blk_start[:, None], axis=1),
                        N_EXPERTS - 1).astype(I32)
    n_used = (pends[-1:] // MOE_BM).astype(I32)
    inv = jnp.full((n_blocks * MOE_BM,), -1, I32).at[dest].set(
        jnp.arange(n, dtype=I32), unique_indices=True, mode="promise_in_bounds")
    inv = inv.reshape(n_blocks, MOE_BM)
    tok = jnp.maximum(inv, 0) // TOP_K
    src3 = tok.reshape(n_blocks, 1, MOE_BM)
    spare = n + (jnp.arange(n_blocks, dtype=I32)[:, None] % 2) * MOE_BM + jnp.arange(MOE_BM, dtype=I32)
    dst = jnp.where(inv >= 0, (inv % TOP_K) * t + tok, spare)
    first = (n + MOE_BM + jnp.arange(MOE_BM, dtype=I32))[None]
    dst3 = jnp.concatenate([first, dst], axis=0).reshape(n_blocks + 1, 1, MOE_BM)

    y = _moe(x2, row(g_ffn), w_e_gate.astype(BF16), w_e_up.astype(BF16), w_e_down.astype(BF16),
             blk_e, n_used, src3, dst3, n_blocks, n + 2 * MOE_BM)
    out = _combine(x2, route, row(g_final), y)
    return out.reshape(batch, seq, d)


def kernel(x, mem, g_mix, w_in, g_gm, w_spatial, b_spatial, w_branch_sb, w_branch_gm, w_out,
           g_cross, g_mem, w_xq, w_xkv, w_xo, g_ffn, w_rg, b_rg, w_re, b_re,
           w_e_gate, w_e_up, w_e_down, g_final):
    assert w_in.shape[0] == 1, "single layer"
    return _layer(x, mem, g_mix[0], w_in[0], g_gm[0], w_spatial[0], b_spatial[0],
                  w_branch_sb[0], w_branch_gm[0], w_out[0], g_cross[0], g_mem[0], w_xq[0],
                  w_xkv[0], w_xo[0], g_ffn[0], w_rg[0], b_rg[0], w_re[0], b_re[0],
                  w_e_gate[0], w_e_up[0], w_e_down[0], g_final)
```

```python
import functools
import math

import jax
import jax.numpy as jnp
from jax import lax
from jax.experimental import pallas as pl
from jax.experimental.pallas import tpu as pltpu

F32 = jnp.float32
BF16 = jnp.bfloat16
I32 = jnp.int32

EPS = 1e-6
LANES = 128
MIB = 1024 * 1024

SB_HEADS = 8
GM_GROUPS = 8
GM_CHUNK = 128
X_HEADS = 4
N_GROUPS = 4
EXPERTS_PER_GROUP = 8
N_EXPERTS = N_GROUPS * EXPERTS_PER_GROUP
TOP_K = 2

INPROJ_BM = 512
INPROJ_BN = 1024
ATTN_QB = 128
ATTN_KEYS = 256
ATTN_CHAINS = 4
MERGE_BM = 256
CROSS_BM = 256
ROUTER_BM = 512
RANK_ROWS = 8
MOE_BM = 256
COMBINE_BM = 256
ATTN_SKIP_LOG = 88.0


def _dot(a, b):
    return jnp.dot(a, b, preferred_element_type=F32)


def _dot_nt(a, b):
    return lax.dot_general(a, b, (((1,), (1,)), ((), ())), preferred_element_type=F32)


def _rms_scale(x):
    return x * lax.rsqrt(jnp.mean(x * x, axis=-1, keepdims=True) + EPS)


def _gelu(x):
    c = math.sqrt(2.0 / math.pi)
    return 0.5 * x * (1.0 + jnp.tanh(c * (x + 0.044715 * (x * x * x))))


def _sigmoid(x):
    return 0.5 * (jnp.tanh(0.5 * x) + 1.0)


def _params(semantics, vmem_mib):
    return pltpu.CompilerParams(dimension_semantics=semantics,
                                vmem_limit_bytes=vmem_mib * MIB)


def _inproj_kernel(x_ref, g_ref, w_ref, ggm_ref, qkv_ref, u_ref, vn_ref, gates_ref, hn_ref):
    j = pl.program_id(1)

    @pl.when(j == 0)
    def _():
        hn_ref[...] = (_rms_scale(x_ref[...]) * g_ref[...]).astype(BF16)

    def acc():
        return _dot(hn_ref[...], w_ref[...])

    @pl.when(j < 3)
    def _():
        qkv_ref[...] = acc().astype(BF16)

    @pl.when(j == 3)
    def _():
        u_ref[...] = _gelu(acc()).astype(BF16)

    @pl.when(j == 4)
    def _():
        vn_ref[...] = (_rms_scale(_gelu(acc())) * ggm_ref[...]).astype(BF16)

    @pl.when(j >= 5)
    def _():
        gates_ref[...] = _sigmoid(acc()).astype(BF16)


def _inproj(x2d, g_mix, w_in_bf, g_gm):
    t, d = x2d.shape
    bm, bn = INPROJ_BM, INPROJ_BN
    ncol = w_in_bf.shape[1] // bn
    assert ncol == 9 and t % bm == 0
    out_shape = (
        jax.ShapeDtypeStruct((t, 3 * bn), BF16),
        jax.ShapeDtypeStruct((t, bn), BF16),
        jax.ShapeDtypeStruct((t, bn), BF16),
        jax.ShapeDtypeStruct((t, 4 * bn), BF16),
    )
    return pl.pallas_call(
        _inproj_kernel,
        out_shape=out_shape,
        grid=(t // bm, ncol),
        in_specs=[
            pl.BlockSpec((bm, d), lambda i, j: (i, 0)),
            pl.BlockSpec((1, d), lambda i, j: (0, 0)),
            pl.BlockSpec((d, bn), lambda i, j: (0, j)),
            pl.BlockSpec((1, bn), lambda i, j: (0, 0)),
        ],
        out_specs=(
            pl.BlockSpec((bm, bn), lambda i, j: (i, jnp.minimum(j, 2))),
            pl.BlockSpec((bm, bn), lambda i, j: (i, 0)),
            pl.BlockSpec((bm, bn), lambda i, j: (i, 0)),
            pl.BlockSpec((bm, bn), lambda i, j: (i, jnp.clip(j - 5, 0, 3))),
        ),
        scratch_shapes=[pltpu.VMEM((bm, d), BF16)],
        compiler_params=_params(("parallel", "arbitrary"), 48),
        name="inproj",
    )(x2d, g_mix, w_in_bf, g_gm)


def _sb_attn_kernel(q_ref, k_ref, v_ref, o_ref, *, scale):
    s, _ = q_ref.shape
    qb = ATTN_QB
    kw = ATTN_KEYS
    rowi = lax.broadcasted_iota(I32, (qb, kw), 0)
    coli = lax.broadcasted_iota(I32, (qb, kw), 1)
    r2 = lax.broadcasted_iota(I32, (kw, kw), 0)
    c2 = lax.broadcasted_iota(I32, (kw, kw), 1)
    above = (r2 > c2).astype(BF16)
    above2 = jnp.concatenate([above, above], axis=0)

    def windows(qs, starts, limits, carries):
        starts = [pl.multiple_of(st, qb) for st in starts]
        zs = [_dot_nt(q, k_ref[pl.ds(st, kw), :]) * scale for q, st in zip(qs, starts)]
        valids, log_sigs, log_nots, splits = [], [], [], []
        for z, st, lim in zip(zs, starts, limits):
            valid = coli + st < lim
            lp = jnp.log1p(jnp.exp(-jnp.abs(z)))
            log_not = jnp.where(valid, -jnp.maximum(z, 0.0) - lp, 0.0)
            hi = log_not.astype(BF16)
            mid = (log_not - hi.astype(F32)).astype(BF16)
            valids.append(valid)
            log_sigs.append(jnp.minimum(z, 0.0) - lp)
            log_nots.append(log_not)
            splits.append(jnp.concatenate([hi, mid], axis=1))
        betweens = [_dot(sp, above2) for sp in splits]
        probs = [jnp.where(valid, jnp.exp(ls + bt + carry), 0.0).astype(BF16)
                 for valid, ls, bt, carry in zip(valids, log_sigs, betweens, carries)]
        outs = [_dot(p, v_ref[pl.ds(st, kw), :]) for p, st in zip(probs, starts)]
        new_carries = [carry + bt[:, 0:1] + ln[:, 0:1]
                       for carry, bt, ln in zip(carries, betweens, log_nots)]
        return outs, new_carries

    chains = ATTN_CHAINS
    per_chain = s // qb // chains

    def qgroup(g, _):
        blocks = [c * per_chain + g for c in range(chains)]
        qs = [q_ref[pl.ds(pl.multiple_of(i * qb, qb), qb), :] for i in blocks]
        starts = [jnp.maximum(i - 1, 0) * qb for i in blocks]
        outs, carries = windows(qs, starts, [i * qb + rowi for i in blocks],
                                [jnp.zeros((qb, 1), F32)] * chains)

        def pending(starts, carries):
            need = [jnp.where(st > 0, jnp.max(c), -jnp.inf) for st, c in zip(starts, carries)]
            return functools.reduce(jnp.maximum, need)

        def cond(st):
            return st[3] > -ATTN_SKIP_LOG

        def body(st):
            starts, outs, carries, _ = st
            nxt = [jnp.maximum(k0 - kw, 0) for k0 in starts]
            more, carries = windows(qs, nxt, starts, carries)
            outs = tuple(o + m for o, m in zip(outs, more))
            return tuple(nxt), outs, tuple(carries), pending(nxt, carries)

        _, outs, _, _ = lax.while_loop(
            cond, body, (tuple(starts), tuple(outs), tuple(carries), pending(starts, carries)))
        for i, o in zip(blocks, outs):
            o_ref[pl.ds(pl.multiple_of(i * qb, qb), qb), :] = o.astype(o_ref.dtype)
        return 0

    lax.fori_loop(0, per_chain, qgroup, 0)


def _sb_attn(qkv, batch, seq):
    width = qkv.shape[1] // 3
    dh = width // SB_HEADS
    assert dh == LANES and seq % (ATTN_QB * ATTN_CHAINS) == 0 and seq >= ATTN_KEYS
    qkv3 = qkv.reshape(batch, seq, 3 * width)
    kern = functools.partial(_sb_attn_kernel, scale=1.0 / math.sqrt(dh))
    blk = (None, seq, dh)
    o = pl.pallas_call(
        kern,
        out_shape=jax.ShapeDtypeStruct((batch, seq, width), BF16),
        grid=(batch, SB_HEADS),
        in_specs=[
            pl.BlockSpec(blk, lambda b, h: (b, 0, h)),
            pl.BlockSpec(blk, lambda b, h: (b, 0, SB_HEADS + h)),
            pl.BlockSpec(blk, lambda b, h: (b, 0, 2 * SB_HEADS + h)),
        ],
        out_specs=pl.BlockSpec(blk, lambda b, h: (b, 0, h)),
        compiler_params=_params(("parallel", "parallel"), 48),
        name="sb_attn",
    )(qkv3, qkv3, qkv3)
    return o.reshape(batch * seq, width)


def _merge_kernel(osb_ref, u_ref, vn_ref, gsb_ref, ggm_ref, wsp_ref, bsp_ref, wsb_ref, wgm_ref,
                  out_ref, ogm_ref):
    bm = osb_ref.shape[0]
    c = GM_CHUNK
    row = lax.broadcasted_iota(I32, (c, c), 0)
    col = lax.broadcasted_iota(I32, (c, c), 1)
    causal = row >= col
    for g in range(GM_GROUPS):
        w = jnp.where(causal, wsp_ref[g], 0.0).astype(BF16)
        gs = slice(g * c, (g + 1) * c)
        for ch in range(bm // c):
            rs = slice(ch * c, (ch + 1) * c)
            mixed = _dot(w, vn_ref[rs, gs]) + bsp_ref[g]
            ogm_ref[rs, gs] = (u_ref[rs, gs].astype(F32) * mixed).astype(BF16)
    merged = (gsb_ref[...].astype(F32) * _dot(osb_ref[...], wsb_ref[...])
              + ggm_ref[...].astype(F32) * _dot(ogm_ref[...], wgm_ref[...]))
    out_ref[...] = merged.astype(BF16)


def _merge(o_sb, u, vn, gates, w_spatial, b_bcast, w_sb_bf, w_gm_bf):
    t, width = o_sb.shape
    d = w_sb_bf.shape[1]
    bm = MERGE_BM
    assert t % bm == 0 and bm % GM_CHUNK == 0 and width == GM_GROUPS * GM_CHUNK
    const3 = lambda i: (0, 0, 0)
    const2 = lambda i: (0, 0)
    return pl.pallas_call(
        _merge_kernel,
        out_shape=jax.ShapeDtypeStruct((t, d), BF16),
        grid=(t // bm,),
        in_specs=[
            pl.BlockSpec((bm, width), lambda i: (i, 0)),
            pl.BlockSpec((bm, width), lambda i: (i, 0)),
            pl.BlockSpec((bm, width), lambda i: (i, 0)),
            pl.BlockSpec((bm, d), lambda i: (i, 0)),
            pl.BlockSpec((bm, d), lambda i: (i, 1)),
            pl.BlockSpec(w_spatial.shape, const3),
            pl.BlockSpec(b_bcast.shape, const3),
            pl.BlockSpec(w_sb_bf.shape, const2),
            pl.BlockSpec(w_gm_bf.shape, const2),
        ],
        out_specs=pl.BlockSpec((bm, d), lambda i: (i, 0)),
        scratch_shapes=[pltpu.VMEM((bm, width), BF16)],
        compiler_params=_params(("parallel",), 48),
        name="merge",
    )(o_sb, u, vn, gates, gates, w_spatial, b_bcast, w_sb_bf, w_gm_bf)


def _memkv_kernel(mem_ref, g_ref, w_ref, kv_ref):
    mn = (_rms_scale(mem_ref[...]) * g_ref[...]).astype(BF16)
    kv_ref[...] = _dot(mn, w_ref[...]).astype(BF16)


def _memkv(mem2d, g_mem, w_xkv_bf):
    rows, d = mem2d.shape
    n = w_xkv_bf.shape[1]
    bm = 256
    assert rows % bm == 0
    return pl.pallas_call(
        _memkv_kernel,
        out_shape=jax.ShapeDtypeStruct((rows, n), BF16),
        grid=(rows // bm,),
        in_specs=[
            pl.BlockSpec((bm, d), lambda i: (i, 0)),
            pl.BlockSpec((1, d), lambda i: (0, 0)),
            pl.BlockSpec((d, n), lambda i: (0, 0)),
        ],
        out_specs=pl.BlockSpec((bm, n), lambda i: (i, 0)),
        compiler_params=_params(("parallel",), 32),
        name="memkv",
    )(mem2d, g_mem, w_xkv_bf)


def _cross_kernel(x_ref, m_ref, wout_ref, gc_ref, wxq_ref, kv_ref, wxo_ref, out_ref):
    xw = wxq_ref.shape[1]
    dh = xw // X_HEADS
    x1 = x_ref[...] + _dot(m_ref[...], wout_ref[...])
    hc = (_rms_scale(x1) * gc_ref[...]).astype(BF16)
    q = _dot(hc, wxq_ref[...]).astype(BF16)
    inv = 1.0 / math.sqrt(dh)
    heads = []
    for h in range(X_HEADS):
        kh = kv_ref[:, h * dh:(h + 1) * dh]
        vh = kv_ref[:, xw + h * dh: xw + (h + 1) * dh]
        s = _dot_nt(q[:, h * dh:(h + 1) * dh], kh) * inv
        e = jnp.exp(s - jnp.max(s, axis=-1, keepdims=True))
        p = e / jnp.sum(e, axis=-1, keepdims=True)
        heads.append(_dot(p.astype(BF16), vh).astype(BF16))
    o = jnp.concatenate(heads, axis=-1)
    out_ref[...] = x1 + _dot(o, wxo_ref[...])


def _cross(x2d, merged, w_out_bf, g_cross, w_xq_bf, kv, w_xo_bf, seq, n_mem):
    t, d = x2d.shape
    bm = CROSS_BM
    assert seq % bm == 0
    steps_per_batch = seq // bm
    const2 = lambda i: (0, 0)
    return pl.pallas_call(
        _cross_kernel,
        out_shape=jax.ShapeDtypeStruct((t, d), F32),
        grid=(t // bm,),
        in_specs=[
            pl.BlockSpec((bm, d), lambda i: (i, 0)),
            pl.BlockSpec((bm, d), lambda i: (i, 0)),
            pl.BlockSpec(w_out_bf.shape, const2),
            pl.BlockSpec((1, d), const2),
            pl.BlockSpec(w_xq_bf.shape, const2),
            pl.BlockSpec((n_mem, kv.shape[1]), lambda i: (i // steps_per_batch, 0)),
            pl.BlockSpec(w_xo_bf.shape, const2),
        ],
        out_specs=pl.BlockSpec((bm, d), lambda i: (i, 0)),
        compiler_params=_params(("parallel",), 48),
        name="cross",
    )(x2d, merged, w_out_bf, g_cross, w_xq_bf, kv, w_xo_bf)


def _router_kernel(x_ref, g_ref, whi_ref, wlo_ref, b_ref, route_ref):
    h = _rms_scale(x_ref[...]) * g_ref[...]
    h_hi = h.astype(BF16)
    h_lo = (h - h_hi.astype(F32)).astype(BF16)
    whi = whi_ref[...]
    lg = _dot(h_hi, whi) + _dot(h_hi, wlo_ref[...]) + _dot(h_lo, whi) + b_ref[...]
    lane = lax.broadcasted_iota(I32, lg.shape, 1)
    neg = -jnp.inf

    def first_max(mask):
        v = jnp.max(jnp.where(mask, lg, neg), axis=-1, keepdims=True)
        hit = jnp.logical_and(mask, lg == v)
        return v, jnp.min(jnp.where(hit, lane, LANES), axis=-1, keepdims=True)

    gmask = lane < N_GROUPS
    gmax, gidx = first_max(gmask)
    p_g = 1.0 / jnp.sum(jnp.where(gmask, jnp.exp(lg - gmax), 0.0), axis=-1, keepdims=True)
    lo = N_GROUPS + EXPERTS_PER_GROUP * gidx
    emask = jnp.logical_and(lane >= lo, lane < lo + EXPERTS_PER_GROUP)
    v1, i1 = first_max(emask)
    v2, i2 = first_max(jnp.logical_and(emask, lane != i1))
    t = jnp.exp(v2 - v1)
    w1 = p_g / (1.0 + t)
    w2 = w1 * t
    e1 = (i1 - N_GROUPS).astype(F32)
    e2 = (i2 - N_GROUPS).astype(F32)
    route_ref[...] = jnp.where(lane == 0, e1,
                               jnp.where(lane == 1, e2,
                                         jnp.where(lane == 2, w1,
                                                   jnp.where(lane == 3, w2, 0.0))))


def _router(x2d, g_ffn, w_hi, w_lo, b_r):
    t, d = x2d.shape
    bm = ROUTER_BM
    const2 = lambda i: (0, 0)
    return pl.pallas_call(
        _router_kernel,
        out_shape=jax.ShapeDtypeStruct((t, LANES), F32),
        grid=(t // bm,),
        in_specs=[
            pl.BlockSpec((bm, d), lambda i: (i, 0)),
            pl.BlockSpec((1, d), const2),
            pl.BlockSpec((d, LANES), const2),
            pl.BlockSpec((d, LANES), const2),
            pl.BlockSpec((1, LANES), const2),
        ],
        out_specs=pl.BlockSpec((bm, LANES), lambda i: (i, 0)),
        compiler_params=_params(("parallel",), 32),
        name="router",
    )(x2d, g_ffn, w_hi, w_lo, b_r)


def _rank_kernel(e_ref, rank_ref, cnt_ref, carry_ref):
    @pl.when(pl.program_id(0) == 0)
    def _():
        carry_ref[...] = jnp.zeros_like(carry_ref)

    ex = lax.broadcasted_iota(I32, (N_EXPERTS, LANES), 0)
    r = lax.broadcasted_iota(I32, (LANES, LANES), 0)
    c = lax.broadcasted_iota(I32, (LANES, LANES), 1)
    before = (r < c).astype(BF16)
    carry = carry_ref[...]
    for i in range(e_ref.shape[0]):
        hit = ex == e_ref[i:i + 1, :]
        onehot = jnp.where(hit, 1.0, 0.0)
        prefix = _dot(onehot.astype(BF16), before) + carry
        rank_ref[i:i + 1, :] = jnp.sum(jnp.where(hit, prefix, 0.0), axis=0, keepdims=True).astype(I32)
        carry = carry + jnp.sum(onehot, axis=1, keepdims=True)
    carry_ref[...] = carry
    cnt_ref[...] = carry


def _rank(e2d):
    rows = e2d.shape[0]
    rr = RANK_ROWS
    assert rows % rr == 0
    return pl.pallas_call(
        _rank_kernel,
        out_shape=(jax.ShapeDtypeStruct((rows, LANES), I32),
                   jax.ShapeDtypeStruct((N_EXPERTS, LANES), F32)),
        grid=(rows // rr,),
        in_specs=[pl.BlockSpec((rr, LANES), lambda i: (i, 0))],
        out_specs=(pl.BlockSpec((rr, LANES), lambda i: (i, 0)),
                   pl.BlockSpec((N_EXPERTS, LANES), lambda i: (0, 0))),
        scratch_shapes=[pltpu.VMEM((N_EXPERTS, LANES), F32)],
        compiler_params=_params(("arbitrary",), 16),
        name="rank",
    )(e2d)


def _moe_kernel(be_ref, nu_ref, src0_ref, src_next_ref, dst_prev_ref, x_hbm, g_ref,
                wg_ref, wu_ref, wd_ref, y_hbm, xbuf, ybuf, xn_ref, gsem, ssem):
    i = pl.program_id(0)
    n_used = nu_ref[0]
    bm = MOE_BM

    def gather(idx_ref, r, s):
        return pltpu.make_async_copy(x_hbm.at[pl.ds(idx_ref[0, 0, r], 1)],
                                     xbuf.at[s, pl.ds(r, 1)], gsem.at[s])

    def scatter(r, s):
        return pltpu.make_async_copy(ybuf.at[s, pl.ds(r, 1)],
                                     y_hbm.at[pl.ds(dst_prev_ref[0, 0, r], 1)], ssem.at[s])

    def wait_rows(make):
        for _ in range(bm):
            make().wait()

    def start_scatters(s):
        for r in range(bm):
            scatter(r, s).start(priority=r % 2)

    @pl.when(i == 0)
    def _():
        ybuf[...] = jnp.zeros_like(ybuf)
        for r in range(bm):
            gather(src0_ref, r, 0).start()

    def step(slot):
        other = 1 - slot

        @pl.when(i <= n_used)
        def _():
            wait_rows(lambda: gather(src0_ref, 0, slot))

        @pl.when(i + 1 <= n_used)
        def _():
            for r in range(bm):
                gather(src_next_ref, r, other).start()
            start_scatters(other)

        @pl.when(i < n_used)
        def _():
            xn_ref[...] = (_rms_scale(xbuf[slot]) * g_ref[...]).astype(BF16)
            xn = xn_ref[...]
            a = _dot(xn, wg_ref[...])
            b = _dot(xn, wu_ref[...])
            h = (a * _sigmoid(a) * b).astype(BF16)
            ybuf[slot] = _dot(h, wd_ref[...])
            wait_rows(lambda: scatter(0, other))

        @pl.when(i == n_used)
        def _():
            start_scatters(other)
            wait_rows(lambda: scatter(0, other))

    parity = lax.rem(i, 2)
    pl.when(parity == 0)(lambda: step(0))
    pl.when(parity == 1)(lambda: step(1))


def _moe(x2d, g_ffn, wg_bf, wu_bf, wd_bf, blk_e, n_used, src3, dst3, n_blocks, n_out_rows):
    d = x2d.shape[1]
    de = wg_bf.shape[2]
    bm = MOE_BM
    last = lambda nu: nu[0] - 1
    smem_blk = lambda imap: pl.BlockSpec((1, 1, bm), imap, memory_space=pltpu.SMEM)
    expert = lambda i, be, nu: (be[jnp.minimum(i, last(nu))], 0, 0)
    return pl.pallas_call(
        _moe_kernel,
        out_shape=jax.ShapeDtypeStruct((n_out_rows, d), F32),
        grid_spec=pltpu.PrefetchScalarGridSpec(
            num_scalar_prefetch=2,
            grid=(n_blocks + 1,),
            in_specs=[
                smem_blk(lambda i, be, nu: (0, 0, 0)),
                smem_blk(lambda i, be, nu: (jnp.minimum(i + 1, last(nu)), 0, 0)),
                smem_blk(lambda i, be, nu: (jnp.minimum(i, nu[0]), 0, 0)),
                pl.BlockSpec(memory_space=pl.ANY),
                pl.BlockSpec((1, d), lambda i, be, nu: (0, 0)),
                pl.BlockSpec((None, d, de), expert),
                pl.BlockSpec((None, d, de), expert),
                pl.BlockSpec((None, de, d), expert),
            ],
            out_specs=pl.BlockSpec(memory_space=pl.ANY),
            scratch_shapes=[
                pltpu.VMEM((2, bm, d), F32),
                pltpu.VMEM((2, bm, d), F32),
                pltpu.VMEM((bm, d), BF16),
                pltpu.SemaphoreType.DMA((2,)),
                pltpu.SemaphoreType.DMA((2,)),
            ],
        ),
        compiler_params=_params(("arbitrary",), 56),
        name="moe",
    )(blk_e, n_used, src3, src3, dst3, x2d, g_ffn, wg_bf, wu_bf, wd_bf)


def _combine_kernel(x_ref, route_ref, g_ref, y1_ref, y2_ref, out_ref):
    w1 = route_ref[:, 2:3]
    w2 = route_ref[:, 3:4]
    x3 = x_ref[...] + (w1 * y1_ref[...] + w2 * y2_ref[...])
    out_ref[...] = _rms_scale(x3) * g_ref[...]


def _combine(x2d, route, g_final, y):
    t, d = x2d.shape
    tb = COMBINE_BM
    steps = t // tb
    return pl.pallas_call(
        _combine_kernel,
        out_shape=jax.ShapeDtypeStruct((t, d), F32),
        grid=(steps,),
        in_specs=[
            pl.BlockSpec((tb, d), lambda i: (i, 0)),
            pl.BlockSpec((tb, LANES), lambda i: (i, 0)),
            pl.BlockSpec((1, d), lambda i: (0, 0)),
            pl.BlockSpec((tb, d), lambda i: (i, 0)),
            pl.BlockSpec((tb, d), lambda i: (steps + i, 0)),
        ],
        out_specs=pl.BlockSpec((tb, d), lambda i: (i, 0)),
        compiler_params=_params(("parallel",), 40),
        name="combine",
    )(x2d, route, g_final, y, y)


def _layer(x, mem, g_mix, w_in, g_gm, w_spatial, b_spatial, w_branch_sb, w_branch_gm, w_out,
           g_cross, g_mem, w_xq, w_xkv, w_xo, g_ffn, w_rg, b_rg, w_re, b_re,
           w_e_gate, w_e_up, w_e_down, g_final):
    batch, seq, d = x.shape
    n_mem = mem.shape[1]
    t = batch * seq
    row = lambda v: v.reshape(1, -1)
    x2d = x.reshape(t, d)

    qkv, u, vn, gates = _inproj(x2d, row(g_mix), w_in.astype(BF16), row(g_gm))
    o_sb = _sb_attn(qkv, batch, seq)
    b_bcast = jnp.broadcast_to(b_spatial[:, :, None], b_spatial.shape + (GM_CHUNK,))
    merged = _merge(o_sb, u, vn, gates, w_spatial, b_bcast,
                    w_branch_sb.astype(BF16), w_branch_gm.astype(BF16))
    kv = _memkv(mem.reshape(batch * n_mem, d), row(g_mem), w_xkv.astype(BF16))
    x2 = _cross(x2d, merged, w_out.astype(BF16), row(g_cross), w_xq.astype(BF16), kv,
                w_xo.astype(BF16), seq, n_mem)

    pad = LANES - N_GROUPS - N_EXPERTS
    w_r = jnp.concatenate([w_rg, w_re, jnp.zeros((d, pad), F32)], axis=1)
    b_r = jnp.concatenate([b_rg, b_re, jnp.zeros((pad,), F32)]).reshape(1, LANES)
    w_r_hi = w_r.astype(BF16)
    w_r_lo = (w_r - w_r_hi.astype(F32)).astype(BF16)
    route = _router(x2, row(g_ffn), w_r_hi, w_r_lo, b_r)

    n = t * TOP_K
    e_flat = route[:, :TOP_K].astype(I32).reshape(n)
    rank2d, cnt = _rank(e_flat.reshape(n // LANES, LANES))
    counts = cnt[:, 0].astype(I32)
    padded = ((counts + MOE_BM - 1) // MOE_BM) * MOE_BM
    pends = jnp.cumsum(padded)
    pstarts = pends - padded
    dest = pstarts[e_flat] + rank2d.reshape(n)
    n_blocks = n // MOE_BM + N_EXPERTS
    blk_start = jnp.arange(n_blocks, dtype=I32) * MOE_BM
    blk_e = jnp.minimum(jnp.sum(pends[None, :] <= blk_start[:, None], axis=1),
                        N_EXPERTS - 1).astype(I32)
    n_used = (pends[-1:] // MOE_BM).astype(I32)
    inv = jnp.full((n_blocks * MOE_BM,), -1, I32).at[dest].set(
        jnp.arange(n, dtype=I32), unique_indices=True, mode="promise_in_bounds")
    inv = inv.reshape(n_blocks, MOE_BM)
    tok = jnp.maximum(inv, 0) // TOP_K
    src3 = tok.reshape(n_blocks, 1, MOE_BM)
    spare = n + (jnp.arange(n_blocks, dtype=I32)[:, None] % 2) * MOE_BM + jnp.arange(MOE_BM, dtype=I32)
    dst = jnp.where(inv >= 0, (inv % TOP_K) * t + tok, spare)
    first = (n + MOE_BM + jnp.arange(MOE_BM, dtype=I32))[None]
    dst3 = jnp.concatenate([first, dst], axis=0).reshape(n_blocks + 1, 1, MOE_BM)

    y = _moe(x2, row(g_ffn), w_e_gate.astype(BF16), w_e_up.astype(BF16), w_e_down.astype(BF16),
             blk_e, n_used, src3, dst3, n_blocks, n + 2 * MOE_BM)
    out = _combine(x2, route, row(g_final), y)
    return out.reshape(batch, seq, d)


def kernel(x, mem, g_mix, w_in, g_gm, w_spatial, b_spatial, w_branch_sb, w_branch_gm, w_out,
           g_cross, g_mem, w_xq, w_xkv, w_xo, g_ffn, w_rg, b_rg, w_re, b_re,
           w_e_gate, w_e_up, w_e_down, g_final):
    assert w_in.shape[0] == 1, "single layer"
    return _layer(x, mem, g_mix[0], w_in[0], g_gm[0], w_spatial[0], b_spatial[0],
                  w_branch_sb[0], w_branch_gm[0], w_out[0], g_cross[0], g_mem[0], w_xq[0],
                  w_xkv[0], w_xo[0], g_ffn[0], w_rg[0], b_rg[0], w_re[0], b_re[0],
                  w_e_gate[0], w_e_up[0], w_e_down[0], g_final)
```

```python
import functools
import math

import jax
import jax.numpy as jnp
from jax import lax
from jax.experimental import pallas as pl
from jax.experimental.pallas import tpu as pltpu

F32 = jnp.float32
BF16 = jnp.bfloat16
I32 = jnp.int32

EPS = 1e-6
LANES = 128
MIB = 1024 * 1024

SB_HEADS = 8
GM_GROUPS = 8
GM_CHUNK = 128
X_HEADS = 4
N_GROUPS = 4
EXPERTS_PER_GROUP = 8
N_EXPERTS = N_GROUPS * EXPERTS_PER_GROUP
TOP_K = 2

INPROJ_BM = 512
INPROJ_BN = 1024
ATTN_QB = 128
ATTN_KEYS = 256
ATTN_CHAINS = 4
MERGE_BM = 256
CROSS_BM = 256
ROUTER_BM = 512
RANK_ROWS = 8
MOE_BM = 256
MOE_RING = 3
COMBINE_BM = 256
ATTN_SKIP_LOG = 88.0


def _dot(a, b):
    return jnp.dot(a, b, preferred_element_type=F32)


def _dot_nt(a, b):
    return lax.dot_general(a, b, (((1,), (1,)), ((), ())), preferred_element_type=F32)


def _rms_scale(x):
    return x * lax.rsqrt(jnp.mean(x * x, axis=-1, keepdims=True) + EPS)


def _gelu(x):
    c = math.sqrt(2.0 / math.pi)
    return 0.5 * x * (1.0 + jnp.tanh(c * (x + 0.044715 * (x * x * x))))


def _sigmoid(x):
    return 0.5 * (jnp.tanh(0.5 * x) + 1.0)


def _params(semantics, vmem_mib):
    return pltpu.CompilerParams(dimension_semantics=semantics,
                                vmem_limit_bytes=vmem_mib * MIB)


def _inproj_kernel(x_ref, g_ref, w_ref, ggm_ref, qkv_ref, u_ref, vn_ref, gates_ref, hn_ref):
    j = pl.program_id(1)

    @pl.when(j == 0)
    def _():
        hn_ref[...] = (_rms_scale(x_ref[...]) * g_ref[...]).astype(BF16)

    def acc():
        return _dot(hn_ref[...], w_ref[...])

    @pl.when(j < 3)
    def _():
        qkv_ref[...] = acc().astype(BF16)

    @pl.when(j == 3)
    def _():
        u_ref[...] = _gelu(acc()).astype(BF16)

    @pl.when(j == 4)
    def _():
        vn_ref[...] = (_rms_scale(_gelu(acc())) * ggm_ref[...]).astype(BF16)

    @pl.when(j >= 5)
    def _():
        gates_ref[...] = _sigmoid(acc()).astype(BF16)


def _inproj(x2d, g_mix, w_in_bf, g_gm):
    t, d = x2d.shape
    bm, bn = INPROJ_BM, INPROJ_BN
    ncol = w_in_bf.shape[1] // bn
    assert ncol == 9 and t % bm == 0
    out_shape = (
        jax.ShapeDtypeStruct((t, 3 * bn), BF16),
        jax.ShapeDtypeStruct((t, bn), BF16),
        jax.ShapeDtypeStruct((t, bn), BF16),
        jax.ShapeDtypeStruct((t, 4 * bn), BF16),
    )
    return pl.pallas_call(
        _inproj_kernel,
        out_shape=out_shape,
        grid=(t // bm, ncol),
        in_specs=[
            pl.BlockSpec((bm, d), lambda i, j: (i, 0)),
            pl.BlockSpec((1, d), lambda i, j: (0, 0)),
            pl.BlockSpec((d, bn), lambda i, j: (0, j)),
            pl.BlockSpec((1, bn), lambda i, j: (0, 0)),
        ],
        out_specs=(
            pl.BlockSpec((bm, bn), lambda i, j: (i, jnp.minimum(j, 2))),
            pl.BlockSpec((bm, bn), lambda i, j: (i, 0)),
            pl.BlockSpec((bm, bn), lambda i, j: (i, 0)),
            pl.BlockSpec((bm, bn), lambda i, j: (i, jnp.clip(j - 5, 0, 3))),
        ),
        scratch_shapes=[pltpu.VMEM((bm, d), BF16)],
        compiler_params=_params(("parallel", "arbitrary"), 48),
        name="inproj",
    )(x2d, g_mix, w_in_bf, g_gm)


def _sb_attn_kernel(q_ref, k_ref, v_ref, o_ref, *, scale):
    s, _ = q_ref.shape
    qb = ATTN_QB
    kw = ATTN_KEYS
    rowi = lax.broadcasted_iota(I32, (qb, kw), 0)
    coli = lax.broadcasted_iota(I32, (qb, kw), 1)
    r2 = lax.broadcasted_iota(I32, (kw, kw), 0)
    c2 = lax.broadcasted_iota(I32, (kw, kw), 1)
    above = (r2 > c2).astype(BF16)
    above2 = jnp.concatenate([above, above], axis=0)

    def windows(qs, starts, limits, carries):
        starts = [pl.multiple_of(st, qb) for st in starts]
        zs = [_dot_nt(q, k_ref[pl.ds(st, kw), :]) * scale for q, st in zip(qs, starts)]
        valids, log_sigs, log_nots, splits = [], [], [], []
        for z, st, lim in zip(zs, starts, limits):
            valid = coli + st < lim
            lp = jnp.log1p(jnp.exp(-jnp.abs(z)))
            log_not = jnp.where(valid, -jnp.maximum(z, 0.0) - lp, 0.0)
            hi = log_not.astype(BF16)
            mid = (log_not - hi.astype(F32)).astype(BF16)
            valids.append(valid)
            log_sigs.append(jnp.minimum(z, 0.0) - lp)
            log_nots.append(log_not)
            splits.append(jnp.concatenate([hi, mid], axis=1))
        betweens = [_dot(sp, above2) for sp in splits]
        probs = [jnp.where(valid, jnp.exp(ls + bt + carry), 0.0).astype(BF16)
                 for valid, ls, bt, carry in zip(valids, log_sigs, betweens, carries)]
        outs = [_dot(p, v_ref[pl.ds(st, kw), :]) for p, st in zip(probs, starts)]
        new_carries = [carry + bt[:, 0:1] + ln[:, 0:1]
                       for carry, bt, ln in zip(carries, betweens, log_nots)]
        return outs, new_carries

    chains = ATTN_CHAINS
    per_chain = s // qb // chains

    def qgroup(g, _):
        blocks = [c * per_chain + g for c in range(chains)]
        qs = [q_ref[pl.ds(pl.multiple_of(i * qb, qb), qb), :] for i in blocks]
        starts = [jnp.maximum(i - 1, 0) * qb for i in blocks]
        outs, carries = windows(qs, starts, [i * qb + rowi for i in blocks],
                                [jnp.zeros((qb, 1), F32)] * chains)

        def pending(starts, carries):
            need = [jnp.where(st > 0, jnp.max(c), -jnp.inf) for st, c in zip(starts, carries)]
            return functools.reduce(jnp.maximum, need)

        def cond(st):
            return st[3] > -ATTN_SKIP_LOG

        def body(st):
            starts, outs, carries, _ = st
            nxt = [jnp.maximum(k0 - kw, 0) for k0 in starts]
            more, carries = windows(qs, nxt, starts, carries)
            outs = tuple(o + m for o, m in zip(outs, more))
            return tuple(nxt), outs, tuple(carries), pending(nxt, carries)

        _, outs, _, _ = lax.while_loop(
            cond, body, (tuple(starts), tuple(outs), tuple(carries), pending(starts, carries)))
        for i, o in zip(blocks, outs):
            o_ref[pl.ds(pl.multiple_of(i * qb, qb), qb), :] = o.astype(o_ref.dtype)
        return 0

    lax.fori_loop(0, per_chain, qgroup, 0)


def _sb_attn(qkv, batch, seq):
    width = qkv.shape[1] // 3
    dh = width // SB_HEADS
    assert dh == LANES and seq % (ATTN_QB * ATTN_CHAINS) == 0 and seq >= ATTN_KEYS
    qkv3 = qkv.reshape(batch, seq, 3 * width)
    kern = functools.partial(_sb_attn_kernel, scale=1.0 / math.sqrt(dh))
    blk = (None, seq, dh)
    o = pl.pallas_call(
        kern,
        out_shape=jax.ShapeDtypeStruct((batch, seq, width), BF16),
        grid=(batch, SB_HEADS),
        in_specs=[
            pl.BlockSpec(blk, lambda b, h: (b, 0, h)),
            pl.BlockSpec(blk, lambda b, h: (b, 0, SB_HEADS + h)),
            pl.BlockSpec(blk, lambda b, h: (b, 0, 2 * SB_HEADS + h)),
        ],
        out_specs=pl.BlockSpec(blk, lambda b, h: (b, 0, h)),
        compiler_params=_params(("parallel", "parallel"), 48),
        name="sb_attn",
    )(qkv3, qkv3, qkv3)
    return o.reshape(batch * seq, width)


def _merge_kernel(osb_ref, u_ref, vn_ref, gsb_ref, ggm_ref, wsp_ref, bsp_ref, wsb_ref, wgm_ref,
                  out_ref, ogm_ref):
    bm = osb_ref.shape[0]
    c = GM_CHUNK
    row = lax.broadcasted_iota(I32, (c, c), 0)
    col = lax.broadcasted_iota(I32, (c, c), 1)
    causal = row >= col
    for g in range(GM_GROUPS):
        w = jnp.where(causal, wsp_ref[g], 0.0).astype(BF16)
        gs = slice(g * c, (g + 1) * c)
        for ch in range(bm // c):
            rs = slice(ch * c, (ch + 1) * c)
            mixed = _dot(w, vn_ref[rs, gs]) + bsp_ref[g]
            ogm_ref[rs, gs] = (u_ref[rs, gs].astype(F32) * mixed).astype(BF16)
    merged = (gsb_ref[...].astype(F32) * _dot(osb_ref[...], wsb_ref[...])
              + ggm_ref[...].astype(F32) * _dot(ogm_ref[...], wgm_ref[...]))
    out_ref[...] = merged.astype(BF16)


def _merge(o_sb, u, vn, gates, w_spatial, b_bcast, w_sb_bf, w_gm_bf):
    t, width = o_sb.shape
    d = w_sb_bf.shape[1]
    bm = MERGE_BM
    assert t % bm == 0 and bm % GM_CHUNK == 0 and width == GM_GROUPS * GM_CHUNK
    const3 = lambda i: (0, 0, 0)
    const2 = lambda i: (0, 0)
    return pl.pallas_call(
        _merge_kernel,
        out_shape=jax.ShapeDtypeStruct((t, d), BF16),
        grid=(t // bm,),
        in_specs=[
            pl.BlockSpec((bm, width), lambda i: (i, 0)),
            pl.BlockSpec((bm, width), lambda i: (i, 0)),
            pl.BlockSpec((bm, width), lambda i: (i, 0)),
            pl.BlockSpec((bm, d), lambda i: (i, 0)),
            pl.BlockSpec((bm, d), lambda i: (i, 1)),
            pl.BlockSpec(w_spatial.shape, const3),
            pl.BlockSpec(b_bcast.shape, const3),
            pl.BlockSpec(w_sb_bf.shape, const2),
            pl.BlockSpec(w_gm_bf.shape, const2),
        ],
        out_specs=pl.BlockSpec((bm, d), lambda i: (i, 0)),
        scratch_shapes=[pltpu.VMEM((bm, width), BF16)],
        compiler_params=_params(("parallel",), 48),
        name="merge",
    )(o_sb, u, vn, gates, gates, w_spatial, b_bcast, w_sb_bf, w_gm_bf)


def _memkv_kernel(mem_ref, g_ref, w_ref, kv_ref):
    mn = (_rms_scale(mem_ref[...]) * g_ref[...]).astype(BF16)
    kv_ref[...] = _dot(mn, w_ref[...]).astype(BF16)


def _memkv(mem2d, g_mem, w_xkv_bf):
    rows, d = mem2d.shape
    n = w_xkv_bf.shape[1]
    bm = 256
    assert rows % bm == 0
    return pl.pallas_call(
        _memkv_kernel,
        out_shape=jax.ShapeDtypeStruct((rows, n), BF16),
        grid=(rows // bm,),
        in_specs=[
            pl.BlockSpec((bm, d), lambda i: (i, 0)),
            pl.BlockSpec((1, d), lambda i: (0, 0)),
            pl.BlockSpec((d, n), lambda i: (0, 0)),
        ],
        out_specs=pl.BlockSpec((bm, n), lambda i: (i, 0)),
        compiler_params=_params(("parallel",), 32),
        name="memkv",
    )(mem2d, g_mem, w_xkv_bf)


def _cross_kernel(x_ref, m_ref, wout_ref, gc_ref, wxq_ref, kv_ref, wxo_ref, out_ref):
    xw = wxq_ref.shape[1]
    dh = xw // X_HEADS
    x1 = x_ref[...] + _dot(m_ref[...], wout_ref[...])
    hc = (_rms_scale(x1) * gc_ref[...]).astype(BF16)
    q = _dot(hc, wxq_ref[...]).astype(BF16)
    inv = 1.0 / math.sqrt(dh)
    heads = []
    for h in range(X_HEADS):
        kh = kv_ref[:, h * dh:(h + 1) * dh]
        vh = kv_ref[:, xw + h * dh: xw + (h + 1) * dh]
        s = _dot_nt(q[:, h * dh:(h + 1) * dh], kh) * inv
        e = jnp.exp(s - jnp.max(s, axis=-1, keepdims=True))
        p = e / jnp.sum(e, axis=-1, keepdims=True)
        heads.append(_dot(p.astype(BF16), vh).astype(BF16))
    o = jnp.concatenate(heads, axis=-1)
    out_ref[...] = x1 + _dot(o, wxo_ref[...])


def _cross(x2d, merged, w_out_bf, g_cross, w_xq_bf, kv, w_xo_bf, seq, n_mem):
    t, d = x2d.shape
    bm = CROSS_BM
    assert seq % bm == 0
    steps_per_batch = seq // bm
    const2 = lambda i: (0, 0)
    return pl.pallas_call(
        _cross_kernel,
        out_shape=jax.ShapeDtypeStruct((t, d), F32),
        grid=(t // bm,),
        in_specs=[
            pl.BlockSpec((bm, d), lambda i: (i, 0)),
            pl.BlockSpec((bm, d), lambda i: (i, 0)),
            pl.BlockSpec(w_out_bf.shape, const2),
            pl.BlockSpec((1, d), const2),
            pl.BlockSpec(w_xq_bf.shape, const2),
            pl.BlockSpec((n_mem, kv.shape[1]), lambda i: (i // steps_per_batch, 0)),
            pl.BlockSpec(w_xo_bf.shape, const2),
        ],
        out_specs=pl.BlockSpec((bm, d), lambda i: (i, 0)),
        compiler_params=_params(("parallel",), 48),
        name="cross",
    )(x2d, merged, w_out_bf, g_cross, w_xq_bf, kv, w_xo_bf)


def _router_kernel(x_ref, g_ref, whi_ref, wlo_ref, b_ref, route_ref):
    h = _rms_scale(x_ref[...]) * g_ref[...]
    h_hi = h.astype(BF16)
    h_lo = (h - h_hi.astype(F32)).astype(BF16)
    whi = whi_ref[...]
    lg = _dot(h_hi, whi) + _dot(h_hi, wlo_ref[...]) + _dot(h_lo, whi) + b_ref[...]
    lane = lax.broadcasted_iota(I32, lg.shape, 1)
    neg = -jnp.inf

    def first_max(mask):
        v = jnp.max(jnp.where(mask, lg, neg), axis=-1, keepdims=True)
        hit = jnp.logical_and(mask, lg == v)
        return v, jnp.min(jnp.where(hit, lane, LANES), axis=-1, keepdims=True)

    gmask = lane < N_GROUPS
    gmax, gidx = first_max(gmask)
    p_g = 1.0 / jnp.sum(jnp.where(gmask, jnp.exp(lg - gmax), 0.0), axis=-1, keepdims=True)
    lo = N_GROUPS + EXPERTS_PER_GROUP * gidx
    emask = jnp.logical_and(lane >= lo, lane < lo + EXPERTS_PER_GROUP)
    v1, i1 = first_max(emask)
    v2, i2 = first_max(jnp.logical_and(emask, lane != i1))
    t = jnp.exp(v2 - v1)
    w1 = p_g / (1.0 + t)
    w2 = w1 * t
    e1 = (i1 - N_GROUPS).astype(F32)
    e2 = (i2 - N_GROUPS).astype(F32)
    route_ref[...] = jnp.where(lane == 0, e1,
                               jnp.where(lane == 1, e2,
                                         jnp.where(lane == 2, w1,
                                                   jnp.where(lane == 3, w2, 0.0))))


def _router(x2d, g_ffn, w_hi, w_lo, b_r):
    t, d = x2d.shape
    bm = ROUTER_BM
    const2 = lambda i: (0, 0)
    return pl.pallas_call(
        _router_kernel,
        out_shape=jax.ShapeDtypeStruct((t, LANES), F32),
        grid=(t // bm,),
        in_specs=[
            pl.BlockSpec((bm, d), lambda i: (i, 0)),
            pl.BlockSpec((1, d), const2),
            pl.BlockSpec((d, LANES), const2),
            pl.BlockSpec((d, LANES), const2),
            pl.BlockSpec((1, LANES), const2),
        ],
        out_specs=pl.BlockSpec((bm, LANES), lambda i: (i, 0)),
        compiler_params=_params(("parallel",), 32),
        name="router",
    )(x2d, g_ffn, w_hi, w_lo, b_r)


def _rank_kernel(e_ref, rank_ref, cnt_ref, carry_ref):
    @pl.when(pl.program_id(0) == 0)
    def _():
        carry_ref[...] = jnp.zeros_like(carry_ref)

    ex = lax.broadcasted_iota(I32, (N_EXPERTS, LANES), 0)
    r = lax.broadcasted_iota(I32, (LANES, LANES), 0)
    c = lax.broadcasted_iota(I32, (LANES, LANES), 1)
    before = (r < c).astype(BF16)
    carry = carry_ref[...]
    for i in range(e_ref.shape[0]):
        hit = ex == e_ref[i:i + 1, :]
        onehot = jnp.where(hit, 1.0, 0.0)
        prefix = _dot(onehot.astype(BF16), before) + carry
        rank_ref[i:i + 1, :] = jnp.sum(jnp.where(hit, prefix, 0.0), axis=0, keepdims=True).astype(I32)
        carry = carry + jnp.sum(onehot, axis=1, keepdims=True)
    carry_ref[...] = carry
    cnt_ref[...] = carry


def _rank(e2d):
    rows = e2d.shape[0]
    rr = RANK_ROWS
    assert rows % rr == 0
    return pl.pallas_call(
        _rank_kernel,
        out_shape=(jax.ShapeDtypeStruct((rows, LANES), I32),
                   jax.ShapeDtypeStruct((N_EXPERTS, LANES), F32)),
        grid=(rows // rr,),
        in_specs=[pl.BlockSpec((rr, LANES), lambda i: (i, 0))],
        out_specs=(pl.BlockSpec((rr, LANES), lambda i: (i, 0)),
                   pl.BlockSpec((N_EXPERTS, LANES), lambda i: (0, 0))),
        scratch_shapes=[pltpu.VMEM((N_EXPERTS, LANES), F32)],
        compiler_params=_params(("arbitrary",), 16),
        name="rank",
    )(e2d)


def _moe_kernel(be_ref, nu_ref, src0_ref, src1_ref, src_next_ref, dst_prev_ref, x_hbm, g_ref,
                wg_ref, wu_ref, wd_ref, y_hbm, xbuf, ybuf, xn_ref, gsem, ssem):
    i = pl.program_id(0)
    n_used = nu_ref[0]
    bm = MOE_BM

    def gather(idx_ref, r, s):
        return pltpu.make_async_copy(x_hbm.at[pl.ds(idx_ref[0, 0, r], 1)],
                                     xbuf.at[s, pl.ds(r, 1)], gsem.at[s])

    def scatter(r, s):
        return pltpu.make_async_copy(ybuf.at[s, pl.ds(r, 1)],
                                     y_hbm.at[pl.ds(dst_prev_ref[0, 0, r], 1)], ssem.at[s])

    def wait_rows(make):
        for _ in range(bm):
            make().wait()

    def start_scatters(s):
        for r in range(bm):
            scatter(r, s).start(priority=r % 2)

    @pl.when(i == 0)
    def _():
        ybuf[...] = jnp.zeros_like(ybuf)
        for r in range(bm):
            gather(src0_ref, r, 0).start()
        for r in range(bm):
            gather(src1_ref, r, 1).start()

    def step(slot):
        ahead = (slot + 2) % MOE_RING
        done = (slot + 1) % MOE_RING

        @pl.when(i <= n_used + 1)
        def _():
            wait_rows(lambda: gather(src0_ref, 0, slot))

        @pl.when(i < n_used)
        def _():
            xn_ref[...] = (_rms_scale(xbuf[slot]) * g_ref[...]).astype(BF16)
            for r in range(bm):
                gather(src_next_ref, r, ahead).start()
            start_scatters(ahead)
            xn = xn_ref[...]
            a = _dot(xn, wg_ref[...])
            b = _dot(xn, wu_ref[...])
            h = (a * _sigmoid(a) * b).astype(BF16)
            ybuf[slot] = _dot(h, wd_ref[...])

        @pl.when(i == n_used)
        def _():
            start_scatters(ahead)

        @pl.when(jnp.logical_and(i >= 1, i <= n_used + 1))
        def _():
            wait_rows(lambda: scatter(0, done))

    phase = lax.rem(i, MOE_RING)
    for slot in range(MOE_RING):
        pl.when(phase == slot)(functools.partial(step, slot))


def _moe(x2d, g_ffn, wg_bf, wu_bf, wd_bf, blk_e, n_used, src3, dst3, n_blocks, n_out_rows):
    d = x2d.shape[1]
    de = wg_bf.shape[2]
    bm = MOE_BM
    last = lambda nu: nu[0] - 1
    smem_blk = lambda imap: pl.BlockSpec((1, 1, bm), imap, memory_space=pltpu.SMEM)
    expert = lambda i, be, nu: (be[jnp.minimum(i, last(nu))], 0, 0)
    return pl.pallas_call(
        _moe_kernel,
        out_shape=jax.ShapeDtypeStruct((n_out_rows, d), F32),
        grid_spec=pltpu.PrefetchScalarGridSpec(
            num_scalar_prefetch=2,
            grid=(n_blocks + 2,),
            in_specs=[
                smem_blk(lambda i, be, nu: (0, 0, 0)),
                smem_blk(lambda i, be, nu: (jnp.minimum(1, last(nu)), 0, 0)),
                smem_blk(lambda i, be, nu: (jnp.minimum(i + 2, last(nu)), 0, 0)),
                smem_blk(lambda i, be, nu: (jnp.minimum(i, nu[0]), 0, 0)),
                pl.BlockSpec(memory_space=pl.ANY),
                pl.BlockSpec((1, d), lambda i, be, nu: (0, 0)),
                pl.BlockSpec((None, d, de), expert),
                pl.BlockSpec((None, d, de), expert),
                pl.BlockSpec((None, de, d), expert),
            ],
            out_specs=pl.BlockSpec(memory_space=pl.ANY),
            scratch_shapes=[
                pltpu.VMEM((MOE_RING, bm, d), F32),
                pltpu.VMEM((MOE_RING, bm, d), F32),
                pltpu.VMEM((bm, d), BF16),
                pltpu.SemaphoreType.DMA((MOE_RING,)),
                pltpu.SemaphoreType.DMA((MOE_RING,)),
            ],
        ),
        compiler_params=_params(("arbitrary",), 56),
        name="moe",
    )(blk_e, n_used, src3, src3, src3, dst3, x2d, g_ffn, wg_bf, wu_bf, wd_bf)


def _combine_kernel(x_ref, route_ref, g_ref, y1_ref, y2_ref, out_ref):
    w1 = route_ref[:, 2:3]
    w2 = route_ref[:, 3:4]
    x3 = x_ref[...] + (w1 * y1_ref[...] + w2 * y2_ref[...])
    out_ref[...] = _rms_scale(x3) * g_ref[...]


def _combine(x2d, route, g_final, y):
    t, d = x2d.shape
    tb = COMBINE_BM
    steps = t // tb
    return pl.pallas_call(
        _combine_kernel,
        out_shape=jax.ShapeDtypeStruct((t, d), F32),
        grid=(steps,),
        in_specs=[
            pl.BlockSpec((tb, d), lambda i: (i, 0)),
            pl.BlockSpec((tb, LANES), lambda i: (i, 0)),
            pl.BlockSpec((1, d), lambda i: (0, 0)),
            pl.BlockSpec((tb, d), lambda i: (i, 0)),
            pl.BlockSpec((tb, d), lambda i: (steps + i, 0)),
        ],
        out_specs=pl.BlockSpec((tb, d), lambda i: (i, 0)),
        compiler_params=_params(("parallel",), 40),
        name="combine",
    )(x2d, route, g_final, y, y)


def _layer(x, mem, g_mix, w_in, g_gm, w_spatial, b_spatial, w_branch_sb, w_branch_gm, w_out,
           g_cross, g_mem, w_xq, w_xkv, w_xo, g_ffn, w_rg, b_rg, w_re, b_re,
           w_e_gate, w_e_up, w_e_down, g_final):
    batch, seq, d = x.shape
    n_mem = mem.shape[1]
    t = batch * seq
    row = lambda v: v.reshape(1, -1)
    x2d = x.reshape(t, d)

    qkv, u, vn, gates = _inproj(x2d, row(g_mix), w_in.astype(BF16), row(g_gm))
    o_sb = _sb_attn(qkv, batch, seq)
    b_bcast = jnp.broadcast_to(b_spatial[:, :, None], b_spatial.shape + (GM_CHUNK,))
    merged = _merge(o_sb, u, vn, gates, w_spatial, b_bcast,
                    w_branch_sb.astype(BF16), w_branch_gm.astype(BF16))
    kv = _memkv(mem.reshape(batch * n_mem, d), row(g_mem), w_xkv.astype(BF16))
    x2 = _cross(x2d, merged, w_out.astype(BF16), row(g_cross), w_xq.astype(BF16), kv,
                w_xo.astype(BF16), seq, n_mem)

    pad = LANES - N_GROUPS - N_EXPERTS
    w_r = jnp.concatenate([w_rg, w_re, jnp.zeros((d, pad), F32)], axis=1)
    b_r = jnp.concatenate([b_rg, b_re, jnp.zeros((pad,), F32)]).reshape(1, LANES)
    w_r_hi = w_r.astype(BF16)
    w_r_lo = (w_r - w_r_hi.astype(F32)).astype(BF16)
    route = _router(x2, row(g_ffn), w_r_hi, w_r_lo, b_r)

    n = t * TOP_K
    e_flat = route[:, :TOP_K].astype(I32).reshape(n)
    rank2d, cnt = _rank(e_flat.reshape(n // LANES, LANES))
    counts = cnt[:, 0].astype(I32)
    padded = ((counts + MOE_BM - 1) // MOE_BM) * MOE_BM
    pends = jnp.cumsum(padded)
    pstarts = pends - padded
    dest = pstarts[e_flat] + rank2d.reshape(n)
    n_blocks = n // MOE_BM + N_EXPERTS
    blk_start = jnp.arange(n_blocks, dtype=I32) * MOE_BM
    blk_e = jnp.minimum(jnp.sum(pends[None, :] <= blk_start[:, None], axis=1),
                        N_EXPERTS - 1).astype(I32)
    n_used = (pends[-1:] // MOE_BM).astype(I32)
    inv = jnp.full((n_blocks * MOE_BM,), -1, I32).at[dest].set(
        jnp.arange(n, dtype=I32), unique_indices=True, mode="promise_in_bounds")
    inv = inv.reshape(n_blocks, MOE_BM)
    tok = jnp.maximum(inv, 0) // TOP_K
    src3 = tok.reshape(n_blocks, 1, MOE_BM)
    spare = n + (jnp.arange(n_blocks, dtype=I32)[:, None] % 2) * MOE_BM + jnp.arange(MOE_BM, dtype=I32)
    dst = jnp.where(inv >= 0, (inv % TOP_K) * t + tok, spare)
    first = (n + MOE_BM + jnp.arange(MOE_BM, dtype=I32))[None]
    dst3 = jnp.concatenate([first, dst], axis=0).reshape(n_blocks + 1, 1, MOE_BM)

    y = _moe(x2, row(g_ffn), w_e_gate.astype(BF16), w_e_up.astype(BF16), w_e_down.astype(BF16),
             blk_e, n_used, src3, dst3, n_blocks, n + 2 * MOE_BM)
    out = _combine(x2, route, row(g_final), y)
    return out.reshape(batch, seq, d)


def kernel(x, mem, g_mix, w_in, g_gm, w_spatial, b_spatial, w_branch_sb, w_branch_gm, w_out,
           g_cross, g_mem, w_xq, w_xkv, w_xo, g_ffn, w_rg, b_rg, w_re, b_re,
           w_e_gate, w_e_up, w_e_down, g_final):
    assert w_in.shape[0] == 1, "single layer"
    return _layer(x, mem, g_mix[0], w_in[0], g_gm[0], w_spatial[0], b_spatial[0],
                  w_branch_sb[0], w_branch_gm[0], w_out[0], g_cross[0], g_mem[0], w_xq[0],
                  w_xkv[0], w_xo[0], g_ffn[0], w_rg[0], b_rg[0], w_re[0], b_re[0],
                  w_e_gate[0], w_e_up[0], w_e_down[0], g_final)
```

```python
import functools
import math

import jax
import jax.numpy as jnp
from jax import lax
from jax.experimental import pallas as pl
from jax.experimental.pallas import tpu as pltpu

F32 = jnp.float32
BF16 = jnp.bfloat16
I32 = jnp.int32

EPS = 1e-6
LANES = 128
MIB = 1024 * 1024

SB_HEADS = 8
GM_GROUPS = 8
GM_CHUNK = 128
X_HEADS = 4
N_GROUPS = 4
EXPERTS_PER_GROUP = 8
N_EXPERTS = N_GROUPS * EXPERTS_PER_GROUP
TOP_K = 2

INPROJ_BM = 512
INPROJ_BN = 1024
ATTN_QB = 128
ATTN_KEYS = 256
ATTN_CHAINS = 4
MERGE_BM = 512
CROSS_BM = 512
ROUTER_BM = 512
RANK_ROWS = 8
MOE_BM = 256
MOE_RING = 3
COMBINE_BM = 256
ATTN_SKIP_LOG = 88.0


def _dot(a, b):
    return jnp.dot(a, b, preferred_element_type=F32)


def _dot_nt(a, b):
    return lax.dot_general(a, b, (((1,), (1,)), ((), ())), preferred_element_type=F32)


def _rms_scale(x):
    return x * lax.rsqrt(jnp.mean(x * x, axis=-1, keepdims=True) + EPS)


def _gelu(x):
    c = math.sqrt(2.0 / math.pi)
    return 0.5 * x * (1.0 + jnp.tanh(c * (x + 0.044715 * (x * x * x))))


def _sigmoid(x):
    return 0.5 * (jnp.tanh(0.5 * x) + 1.0)


def _resident(shape):
    return pl.BlockSpec(shape, lambda i: (0,) * len(shape), pipeline_mode=pl.Buffered(1))


def _params(semantics, vmem_mib):
    return pltpu.CompilerParams(dimension_semantics=semantics,
                                vmem_limit_bytes=vmem_mib * MIB)


def _inproj_kernel(x_ref, g_ref, w_ref, ggm_ref, qkv_ref, u_ref, vn_ref, gates_ref, hn_ref):
    j = pl.program_id(1)

    @pl.when(j == 0)
    def _():
        hn_ref[...] = (_rms_scale(x_ref[...]) * g_ref[...]).astype(BF16)

    def acc():
        return _dot(hn_ref[...], w_ref[...])

    @pl.when(j < 3)
    def _():
        qkv_ref[...] = acc().astype(BF16)

    @pl.when(j == 3)
    def _():
        u_ref[...] = _gelu(acc()).astype(BF16)

    @pl.when(j == 4)
    def _():
        vn_ref[...] = (_rms_scale(_gelu(acc())) * ggm_ref[...]).astype(BF16)

    @pl.when(j >= 5)
    def _():
        gates_ref[...] = _sigmoid(acc()).astype(BF16)


def _inproj(x2d, g_mix, w_in_bf, g_gm):
    t, d = x2d.shape
    bm, bn = INPROJ_BM, INPROJ_BN
    ncol = w_in_bf.shape[1] // bn
    assert ncol == 9 and t % bm == 0
    out_shape = (
        jax.ShapeDtypeStruct((t, 3 * bn), BF16),
        jax.ShapeDtypeStruct((t, bn), BF16),
        jax.ShapeDtypeStruct((t, bn), BF16),
        jax.ShapeDtypeStruct((t, 4 * bn), BF16),
    )
    return pl.pallas_call(
        _inproj_kernel,
        out_shape=out_shape,
        grid=(t // bm, ncol),
        in_specs=[
            pl.BlockSpec((bm, d), lambda i, j: (i, 0)),
            pl.BlockSpec((1, d), lambda i, j: (0, 0)),
            pl.BlockSpec((d, bn), lambda i, j: (0, j)),
            pl.BlockSpec((1, bn), lambda i, j: (0, 0)),
        ],
        out_specs=(
            pl.BlockSpec((bm, bn), lambda i, j: (i, jnp.minimum(j, 2))),
            pl.BlockSpec((bm, bn), lambda i, j: (i, 0)),
            pl.BlockSpec((bm, bn), lambda i, j: (i, 0)),
            pl.BlockSpec((bm, bn), lambda i, j: (i, jnp.clip(j - 5, 0, 3))),
        ),
        scratch_shapes=[pltpu.VMEM((bm, d), BF16)],
        compiler_params=_params(("parallel", "arbitrary"), 48),
        name="inproj",
    )(x2d, g_mix, w_in_bf, g_gm)


def _sb_attn_kernel(q_ref, k_ref, v_ref, o_ref, *, scale):
    s, _ = q_ref.shape
    qb = ATTN_QB
    kw = ATTN_KEYS
    rowi = lax.broadcasted_iota(I32, (qb, kw), 0)
    coli = lax.broadcasted_iota(I32, (qb, kw), 1)
    r2 = lax.broadcasted_iota(I32, (kw, kw), 0)
    c2 = lax.broadcasted_iota(I32, (kw, kw), 1)
    above = (r2 > c2).astype(BF16)
    above2 = jnp.concatenate([above, above], axis=0)

    def windows(qs, starts, limits, carries):
        starts = [pl.multiple_of(st, qb) for st in starts]
        zs = [_dot_nt(q, k_ref[pl.ds(st, kw), :]) * scale for q, st in zip(qs, starts)]
        valids, log_sigs, log_nots, splits = [], [], [], []
        for z, st, lim in zip(zs, starts, limits):
            valid = coli + st < lim
            lp = jnp.log1p(jnp.exp(-jnp.abs(z)))
            log_not = jnp.where(valid, -jnp.maximum(z, 0.0) - lp, 0.0)
            hi = log_not.astype(BF16)
            mid = (log_not - hi.astype(F32)).astype(BF16)
            valids.append(valid)
            log_sigs.append(jnp.minimum(z, 0.0) - lp)
            log_nots.append(log_not)
            splits.append(jnp.concatenate([hi, mid], axis=1))
        betweens = [_dot(sp, above2) for sp in splits]
        probs = [jnp.where(valid, jnp.exp(ls + bt + carry), 0.0).astype(BF16)
                 for valid, ls, bt, carry in zip(valids, log_sigs, betweens, carries)]
        outs = [_dot(p, v_ref[pl.ds(st, kw), :]) for p, st in zip(probs, starts)]
        new_carries = [carry + bt[:, 0:1] + ln[:, 0:1]
                       for carry, bt, ln in zip(carries, betweens, log_nots)]
        return outs, new_carries

    chains = ATTN_CHAINS
    per_chain = s // qb // chains

    def qgroup(g, _):
        blocks = [c * per_chain + g for c in range(chains)]
        qs = [q_ref[pl.ds(pl.multiple_of(i * qb, qb), qb), :] for i in blocks]
        starts = [jnp.maximum(i - 1, 0) * qb for i in blocks]
        outs, carries = windows(qs, starts, [i * qb + rowi for i in blocks],
                                [jnp.zeros((qb, 1), F32)] * chains)

        def pending(starts, carries):
            need = [jnp.where(st > 0, jnp.max(c), -jnp.inf) for st, c in zip(starts, carries)]
            return functools.reduce(jnp.maximum, need)

        def cond(st):
            return st[3] > -ATTN_SKIP_LOG

        def body(st):
            starts, outs, carries, _ = st
            nxt = [jnp.maximum(k0 - kw, 0) for k0 in starts]
            more, carries = windows(qs, nxt, starts, carries)
            outs = tuple(o + m for o, m in zip(outs, more))
            return tuple(nxt), outs, tuple(carries), pending(nxt, carries)

        _, outs, _, _ = lax.while_loop(
            cond, body, (tuple(starts), tuple(outs), tuple(carries), pending(starts, carries)))
        for i, o in zip(blocks, outs):
            o_ref[pl.ds(pl.multiple_of(i * qb, qb), qb), :] = o.astype(o_ref.dtype)
        return 0

    lax.fori_loop(0, per_chain, qgroup, 0)


def _sb_attn(qkv, batch, seq):
    width = qkv.shape[1] // 3
    dh = width // SB_HEADS
    assert dh == LANES and seq % (ATTN_QB * ATTN_CHAINS) == 0 and seq >= ATTN_KEYS
    qkv3 = qkv.reshape(batch, seq, 3 * width)
    kern = functools.partial(_sb_attn_kernel, scale=1.0 / math.sqrt(dh))
    blk = (None, seq, dh)
    o = pl.pallas_call(
        kern,
        out_shape=jax.ShapeDtypeStruct((batch, seq, width), BF16),
        grid=(batch, SB_HEADS),
        in_specs=[
            pl.BlockSpec(blk, lambda b, h: (b, 0, h)),
            pl.BlockSpec(blk, lambda b, h: (b, 0, SB_HEADS + h)),
            pl.BlockSpec(blk, lambda b, h: (b, 0, 2 * SB_HEADS + h)),
        ],
        out_specs=pl.BlockSpec(blk, lambda b, h: (b, 0, h)),
        compiler_params=_params(("parallel", "parallel"), 48),
        name="sb_attn",
    )(qkv3, qkv3, qkv3)
    return o.reshape(batch * seq, width)


def _merge_kernel(osb_ref, u_ref, vn_ref, gsb_ref, ggm_ref, wsp_ref, bsp_ref, wsb_ref, wgm_ref,
                  out_ref, ogm_ref):
    bm = osb_ref.shape[0]
    c = GM_CHUNK
    row = lax.broadcasted_iota(I32, (c, c), 0)
    col = lax.broadcasted_iota(I32, (c, c), 1)
    causal = row >= col
    for g in range(GM_GROUPS):
        w = jnp.where(causal, wsp_ref[g], 0.0).astype(BF16)
        gs = slice(g * c, (g + 1) * c)
        for ch in range(bm // c):
            rs = slice(ch * c, (ch + 1) * c)
            mixed = _dot(w, vn_ref[rs, gs]) + bsp_ref[g]
            ogm_ref[rs, gs] = (u_ref[rs, gs].astype(F32) * mixed).astype(BF16)
    merged = (gsb_ref[...].astype(F32) * _dot(osb_ref[...], wsb_ref[...])
              + ggm_ref[...].astype(F32) * _dot(ogm_ref[...], wgm_ref[...]))
    out_ref[...] = merged.astype(BF16)


def _merge(o_sb, u, vn, gates, w_spatial, b_bcast, w_sb_bf, w_gm_bf):
    t, width = o_sb.shape
    d = w_sb_bf.shape[1]
    bm = MERGE_BM
    assert t % bm == 0 and bm % GM_CHUNK == 0 and width == GM_GROUPS * GM_CHUNK
    return pl.pallas_call(
        _merge_kernel,
        out_shape=jax.ShapeDtypeStruct((t, d), BF16),
        grid=(t // bm,),
        in_specs=[
            pl.BlockSpec((bm, width), lambda i: (i, 0)),
            pl.BlockSpec((bm, width), lambda i: (i, 0)),
            pl.BlockSpec((bm, width), lambda i: (i, 0)),
            pl.BlockSpec((bm, d), lambda i: (i, 0)),
            pl.BlockSpec((bm, d), lambda i: (i, 1)),
            _resident(w_spatial.shape),
            _resident(b_bcast.shape),
            _resident(w_sb_bf.shape),
            _resident(w_gm_bf.shape),
        ],
        out_specs=pl.BlockSpec((bm, d), lambda i: (i, 0)),
        scratch_shapes=[pltpu.VMEM((bm, width), BF16)],
        compiler_params=_params(("parallel",), 48),
        name="merge",
    )(o_sb, u, vn, gates, gates, w_spatial, b_bcast, w_sb_bf, w_gm_bf)


def _memkv_kernel(mem_ref, g_ref, w_ref, kv_ref):
    mn = (_rms_scale(mem_ref[...]) * g_ref[...]).astype(BF16)
    kv_ref[...] = _dot(mn, w_ref[...]).astype(BF16)


def _memkv(mem2d, g_mem, w_xkv_bf):
    rows, d = mem2d.shape
    n = w_xkv_bf.shape[1]
    bm = 256
    assert rows % bm == 0
    return pl.pallas_call(
        _memkv_kernel,
        out_shape=jax.ShapeDtypeStruct((rows, n), BF16),
        grid=(rows // bm,),
        in_specs=[
            pl.BlockSpec((bm, d), lambda i: (i, 0)),
            pl.BlockSpec((1, d), lambda i: (0, 0)),
            pl.BlockSpec((d, n), lambda i: (0, 0)),
        ],
        out_specs=pl.BlockSpec((bm, n), lambda i: (i, 0)),
        compiler_params=_params(("parallel",), 32),
        name="memkv",
    )(mem2d, g_mem, w_xkv_bf)


def _cross_kernel(x_ref, m_ref, wout_ref, gc_ref, wxq_ref, kv_ref, wxo_ref, out_ref):
    xw = wxq_ref.shape[1]
    dh = xw // X_HEADS
    x1 = x_ref[...] + _dot(m_ref[...], wout_ref[...])
    hc = (_rms_scale(x1) * gc_ref[...]).astype(BF16)
    q = _dot(hc, wxq_ref[...]).astype(BF16)
    inv = 1.0 / math.sqrt(dh)
    heads = []
    for h in range(X_HEADS):
        kh = kv_ref[:, h * dh:(h + 1) * dh]
        vh = kv_ref[:, xw + h * dh: xw + (h + 1) * dh]
        s = _dot_nt(q[:, h * dh:(h + 1) * dh], kh) * inv
        e = jnp.exp(s - jnp.max(s, axis=-1, keepdims=True))
        p = e / jnp.sum(e, axis=-1, keepdims=True)
        heads.append(_dot(p.astype(BF16), vh).astype(BF16))
    o = jnp.concatenate(heads, axis=-1)
    out_ref[...] = x1 + _dot(o, wxo_ref[...])


def _cross(x2d, merged, w_out_bf, g_cross, w_xq_bf, kv, w_xo_bf, seq, n_mem):
    t, d = x2d.shape
    bm = CROSS_BM
    assert seq % bm == 0
    steps_per_batch = seq // bm
    const2 = lambda i: (0, 0)
    return pl.pallas_call(
        _cross_kernel,
        out_shape=jax.ShapeDtypeStruct((t, d), F32),
        grid=(t // bm,),
        in_specs=[
            pl.BlockSpec((bm, d), lambda i: (i, 0)),
            pl.BlockSpec((bm, d), lambda i: (i, 0)),
            _resident(w_out_bf.shape),
            pl.BlockSpec((1, d), const2),
            _resident(w_xq_bf.shape),
            pl.BlockSpec((n_mem, kv.shape[1]), lambda i: (i // steps_per_batch, 0)),
            _resident(w_xo_bf.shape),
        ],
        out_specs=pl.BlockSpec((bm, d), lambda i: (i, 0)),
        compiler_params=_params(("parallel",), 48),
        name="cross",
    )(x2d, merged, w_out_bf, g_cross, w_xq_bf, kv, w_xo_bf)


def _router_kernel(x_ref, g_ref, whi_ref, wlo_ref, b_ref, route_ref):
    h = _rms_scale(x_ref[...]) * g_ref[...]
    h_hi = h.astype(BF16)
    h_lo = (h - h_hi.astype(F32)).astype(BF16)
    whi = whi_ref[...]
    lg = _dot(h_hi, whi) + _dot(h_hi, wlo_ref[...]) + _dot(h_lo, whi) + b_ref[...]
    lane = lax.broadcasted_iota(I32, lg.shape, 1)
    neg = -jnp.inf

    def first_max(mask):
        v = jnp.max(jnp.where(mask, lg, neg), axis=-1, keepdims=True)
        hit = jnp.logical_and(mask, lg == v)
        return v, jnp.min(jnp.where(hit, lane, LANES), axis=-1, keepdims=True)

    gmask = lane < N_GROUPS
    gmax, gidx = first_max(gmask)
    p_g = 1.0 / jnp.sum(jnp.where(gmask, jnp.exp(lg - gmax), 0.0), axis=-1, keepdims=True)
    lo = N_GROUPS + EXPERTS_PER_GROUP * gidx
    emask = jnp.logical_and(lane >= lo, lane < lo + EXPERTS_PER_GROUP)
    v1, i1 = first_max(emask)
    v2, i2 = first_max(jnp.logical_and(emask, lane != i1))
    t = jnp.exp(v2 - v1)
    w1 = p_g / (1.0 + t)
    w2 = w1 * t
    e1 = (i1 - N_GROUPS).astype(F32)
    e2 = (i2 - N_GROUPS).astype(F32)
    route_ref[...] = jnp.where(lane == 0, e1,
                               jnp.where(lane == 1, e2,
                                         jnp.where(lane == 2, w1,
                                                   jnp.where(lane == 3, w2, 0.0))))


def _router(x2d, g_ffn, w_hi, w_lo, b_r):
    t, d = x2d.shape
    bm = ROUTER_BM
    const2 = lambda i: (0, 0)
    return pl.pallas_call(
        _router_kernel,
        out_shape=jax.ShapeDtypeStruct((t, LANES), F32),
        grid=(t // bm,),
        in_specs=[
            pl.BlockSpec((bm, d), lambda i: (i, 0)),
            pl.BlockSpec((1, d), const2),
            pl.BlockSpec((d, LANES), const2),
            pl.BlockSpec((d, LANES), const2),
            pl.BlockSpec((1, LANES), const2),
        ],
        out_specs=pl.BlockSpec((bm, LANES), lambda i: (i, 0)),
        compiler_params=_params(("parallel",), 32),
        name="router",
    )(x2d, g_ffn, w_hi, w_lo, b_r)


def _rank_kernel(e_ref, rank_ref, cnt_ref, carry_ref):
    @pl.when(pl.program_id(0) == 0)
    def _():
        carry_ref[...] = jnp.zeros_like(carry_ref)

    ex = lax.broadcasted_iota(I32, (N_EXPERTS, LANES), 0)
    r = lax.broadcasted_iota(I32, (LANES, LANES), 0)
    c = lax.broadcasted_iota(I32, (LANES, LANES), 1)
    before = (r < c).astype(BF16)
    carry = carry_ref[...]
    for i in range(e_ref.shape[0]):
        hit = ex == e_ref[i:i + 1, :]
        onehot = jnp.where(hit, 1.0, 0.0)
        prefix = _dot(onehot.astype(BF16), before) + carry
        rank_ref[i:i + 1, :] = jnp.sum(jnp.where(hit, prefix, 0.0), axis=0, keepdims=True).astype(I32)
        carry = carry + jnp.sum(onehot, axis=1, keepdims=True)
    carry_ref[...] = carry
    cnt_ref[...] = carry


def _rank(e2d):
    rows = e2d.shape[0]
    rr = RANK_ROWS
    assert rows % rr == 0
    return pl.pallas_call(
        _rank_kernel,
        out_shape=(jax.ShapeDtypeStruct((rows, LANES), I32),
                   jax.ShapeDtypeStruct((N_EXPERTS, LANES), F32)),
        grid=(rows // rr,),
        in_specs=[pl.BlockSpec((rr, LANES), lambda i: (i, 0))],
        out_specs=(pl.BlockSpec((rr, LANES), lambda i: (i, 0)),
                   pl.BlockSpec((N_EXPERTS, LANES), lambda i: (0, 0))),
        scratch_shapes=[pltpu.VMEM((N_EXPERTS, LANES), F32)],
        compiler_params=_params(("arbitrary",), 16),
        name="rank",
    )(e2d)


def _moe_kernel(be_ref, nu_ref, src0_ref, src1_ref, src_next_ref, dst_prev_ref, x_hbm, g_ref,
                wg_ref, wu_ref, wd_ref, y_hbm, xbuf, ybuf, xn_ref, gsem, ssem):
    i = pl.program_id(0)
    n_used = nu_ref[0]
    bm = MOE_BM

    def gather(idx_ref, r, s):
        return pltpu.make_async_copy(x_hbm.at[pl.ds(idx_ref[0, 0, r], 1)],
                                     xbuf.at[s, pl.ds(r, 1)], gsem.at[s])

    def scatter(r, s):
        return pltpu.make_async_copy(ybuf.at[s, pl.ds(r, 1)],
                                     y_hbm.at[pl.ds(dst_prev_ref[0, 0, r], 1)], ssem.at[s])

    def wait_rows(make):
        for _ in range(bm):
            make().wait()

    def start_scatters(s):
        for r in range(bm):
            scatter(r, s).start(priority=r % 2)

    @pl.when(i == 0)
    def _():
        ybuf[...] = jnp.zeros_like(ybuf)
        for r in range(bm):
            gather(src0_ref, r, 0).start()
        for r in range(bm):
            gather(src1_ref, r, 1).start()

    def step(slot):
        ahead = (slot + 2) % MOE_RING
        done = (slot + 1) % MOE_RING

        @pl.when(i <= n_used + 1)
        def _():
            wait_rows(lambda: gather(src0_ref, 0, slot))

        @pl.when(i < n_used)
        def _():
            xn_ref[...] = (_rms_scale(xbuf[slot]) * g_ref[...]).astype(BF16)
            for r in range(bm):
                gather(src_next_ref, r, ahead).start(priority=r % 2)
            start_scatters(ahead)
            xn = xn_ref[...]
            a = _dot(xn, wg_ref[...])
            b = _dot(xn, wu_ref[...])
            h = (a * _sigmoid(a) * b).astype(BF16)
            ybuf[slot] = _dot(h, wd_ref[...])

        @pl.when(i == n_used)
        def _():
            start_scatters(ahead)

        @pl.when(jnp.logical_and(i >= 1, i <= n_used + 1))
        def _():
            wait_rows(lambda: scatter(0, done))

    phase = lax.rem(i, MOE_RING)
    for slot in range(MOE_RING):
        pl.when(phase == slot)(functools.partial(step, slot))


def _moe(x2d, g_ffn, wg_bf, wu_bf, wd_bf, blk_e, n_used, src3, dst3, n_blocks, n_out_rows):
    d = x2d.shape[1]
    de = wg_bf.shape[2]
    bm = MOE_BM
    last = lambda nu: nu[0] - 1
    smem_blk = lambda imap: pl.BlockSpec((1, 1, bm), imap, memory_space=pltpu.SMEM)
    expert = lambda i, be, nu: (be[jnp.minimum(i, last(nu))], 0, 0)
    return pl.pallas_call(
        _moe_kernel,
        out_shape=jax.ShapeDtypeStruct((n_out_rows, d), F32),
        grid_spec=pltpu.PrefetchScalarGridSpec(
            num_scalar_prefetch=2,
            grid=(n_blocks + 2,),
            in_specs=[
                smem_blk(lambda i, be, nu: (0, 0, 0)),
                smem_blk(lambda i, be, nu: (jnp.minimum(1, last(nu)), 0, 0)),
                smem_blk(lambda i, be, nu: (jnp.minimum(i + 2, last(nu)), 0, 0)),
                smem_blk(lambda i, be, nu: (jnp.minimum(i, nu[0]), 0, 0)),
                pl.BlockSpec(memory_space=pl.ANY),
                pl.BlockSpec((1, d), lambda i, be, nu: (0, 0)),
                pl.BlockSpec((None, d, de), expert),
                pl.BlockSpec((None, d, de), expert),
                pl.BlockSpec((None, de, d), expert),
            ],
            out_specs=pl.BlockSpec(memory_space=pl.ANY),
            scratch_shapes=[
                pltpu.VMEM((MOE_RING, bm, d), F32),
                pltpu.VMEM((MOE_RING, bm, d), F32),
                pltpu.VMEM((bm, d), BF16),
                pltpu.SemaphoreType.DMA((MOE_RING,)),
                pltpu.SemaphoreType.DMA((MOE_RING,)),
            ],
        ),
        compiler_params=_params(("arbitrary",), 56),
        name="moe",
    )(blk_e, n_used, src3, src3, src3, dst3, x2d, g_ffn, wg_bf, wu_bf, wd_bf)


def _combine_kernel(x_ref, route_ref, g_ref, y1_ref, y2_ref, out_ref):
    w1 = route_ref[:, 2:3]
    w2 = route_ref[:, 3:4]
    x3 = x_ref[...] + (w1 * y1_ref[...] + w2 * y2_ref[...])
    out_ref[...] = _rms_scale(x3) * g_ref[...]


def _combine(x2d, route, g_final, y):
    t, d = x2d.shape
    tb = COMBINE_BM
    steps = t // tb
    return pl.pallas_call(
        _combine_kernel,
        out_shape=jax.ShapeDtypeStruct((t, d), F32),
        grid=(steps,),
        in_specs=[
            pl.BlockSpec((tb, d), lambda i: (i, 0)),
            pl.BlockSpec((tb, LANES), lambda i: (i, 0)),
            pl.BlockSpec((1, d), lambda i: (0, 0)),
            pl.BlockSpec((tb, d), lambda i: (i, 0)),
            pl.BlockSpec((tb, d), lambda i: (steps + i, 0)),
        ],
        out_specs=pl.BlockSpec((tb, d), lambda i: (i, 0)),
        compiler_params=_params(("parallel",), 40),
        name="combine",
    )(x2d, route, g_final, y, y)


def _layer(x, mem, g_mix, w_in, g_gm, w_spatial, b_spatial, w_branch_sb, w_branch_gm, w_out,
           g_cross, g_mem, w_xq, w_xkv, w_xo, g_ffn, w_rg, b_rg, w_re, b_re,
           w_e_gate, w_e_up, w_e_down, g_final):
    batch, seq, d = x.shape
    n_mem = mem.shape[1]
    t = batch * seq
    row = lambda v: v.reshape(1, -1)
    x2d = x.reshape(t, d)

    qkv, u, vn, gates = _inproj(x2d, row(g_mix), w_in.astype(BF16), row(g_gm))
    o_sb = _sb_attn(qkv, batch, seq)
    b_bcast = jnp.broadcast_to(b_spatial[:, :, None], b_spatial.shape + (GM_CHUNK,))
    merged = _merge(o_sb, u, vn, gates, w_spatial, b_bcast,
                    w_branch_sb.astype(BF16), w_branch_gm.astype(BF16))
    kv = _memkv(mem.reshape(batch * n_mem, d), row(g_mem), w_xkv.astype(BF16))
    x2 = _cross(x2d, merged, w_out.astype(BF16), row(g_cross), w_xq.astype(BF16), kv,
                w_xo.astype(BF16), seq, n_mem)

    pad = LANES - N_GROUPS - N_EXPERTS
    w_r = jnp.concatenate([w_rg, w_re, jnp.zeros((d, pad), F32)], axis=1)
    b_r = jnp.concatenate([b_rg, b_re, jnp.zeros((pad,), F32)]).reshape(1, LANES)
    w_r_hi = w_r.astype(BF16)
    w_r_lo = (w_r - w_r_hi.astype(F32)).astype(BF16)
    route = _router(x2, row(g_ffn), w_r_hi, w_r_lo, b_r)

    n = t * TOP_K
    e_flat = route[:, :TOP_K].astype(I32).reshape(n)
    rank2d, cnt = _rank(e_flat.reshape(n // LANES, LANES))
    counts = cnt[:, 0].astype(I32)
    padded = ((counts + MOE_BM - 1) // MOE_BM) * MOE_BM
    pends = jnp.cumsum(padded)
    pstarts = pends - padded
    dest = pstarts[e_flat] + rank2d.reshape(n)
    n_blocks = n // MOE_BM + N_EXPERTS
    blk_start = jnp.arange(n_blocks, dtype=I32) * MOE_BM
    blk_e = jnp.minimum(jnp.sum(pends[None, :] <= blk_start[:, None], axis=1),
                        N_EXPERTS - 1).astype(I32)
    n_used = (pends[-1:] // MOE_BM).astype(I32)
    inv = jnp.full((n_blocks * MOE_BM,), -1, I32).at[dest].set(
        jnp.arange(n, dtype=I32), unique_indices=True, mode="promise_in_bounds")
    inv = inv.reshape(n_blocks, MOE_BM)
    tok = jnp.maximum(inv, 0) // TOP_K
    src3 = tok.reshape(n_blocks, 1, MOE_BM)
    spare = n + (jnp.arange(n_blocks, dtype=I32)[:, None] % 2) * MOE_BM + jnp.arange(MOE_BM, dtype=I32)
    dst = jnp.where(inv >= 0, (inv % TOP_K) * t + tok, spare)
    first = (n + MOE_BM + jnp.arange(MOE_BM, dtype=I32))[None]
    dst3 = jnp.concatenate([first, dst], axis=0).reshape(n_blocks + 1, 1, MOE_BM)

    y = _moe(x2, row(g_ffn), w_e_gate.astype(BF16), w_e_up.astype(BF16), w_e_down.astype(BF16),
             blk_e, n_used, src3, dst3, n_blocks, n + 2 * MOE_BM)
    out = _combine(x2, route, row(g_final), y)
    return out.reshape(batch, seq, d)


def kernel(x, mem, g_mix, w_in, g_gm, w_spatial, b_spatial, w_branch_sb, w_branch_gm, w_out,
           g_cross, g_mem, w_xq, w_xkv, w_xo, g_ffn, w_rg, b_rg, w_re, b_re,
           w_e_gate, w_e_up, w_e_down, g_final):
    assert w_in.shape[0] == 1, "single layer"
    return _layer(x, mem, g_mix[0], w_in[0], g_gm[0], w_spatial[0], b_spatial[0],
                  w_branch_sb[0], w_branch_gm[0], w_out[0], g_cross[0], g_mem[0], w_xq[0],
                  w_xkv[0], w_xo[0], g_ffn[0], w_rg[0], b_rg[0], w_re[0], b_re[0],
                  w_e_gate[0], w_e_up[0], w_e_down[0], g_final)
```

```python
import functools
import math

import jax
import jax.numpy as jnp
from jax import lax
from jax.experimental import pallas as pl
from jax.experimental.pallas import tpu as pltpu

F32 = jnp.float32
BF16 = jnp.bfloat16
I32 = jnp.int32

EPS = 1e-6
LANES = 128
MIB = 1024 * 1024

SB_HEADS = 8
GM_GROUPS = 8
GM_CHUNK = 128
X_HEADS = 4
N_GROUPS = 4
EXPERTS_PER_GROUP = 8
N_EXPERTS = N_GROUPS * EXPERTS_PER_GROUP
TOP_K = 2

INPROJ_BM = 512
INPROJ_BN = 1024
ATTN_QB = 128
ATTN_KEYS = 256
ATTN_CHAINS = 4
MERGE_BM = 512
CROSS_BM = 512
RANK_ROWS = 8
MOE_BM = 256
MOE_RING = 3
COMBINE_BM = 256
ATTN_SKIP_LOG = 88.0


def _dot(a, b):
    return jnp.dot(a, b, preferred_element_type=F32)


def _dot_nt(a, b):
    return lax.dot_general(a, b, (((1,), (1,)), ((), ())), preferred_element_type=F32)


def _rms_scale(x):
    return x * lax.rsqrt(jnp.mean(x * x, axis=-1, keepdims=True) + EPS)


def _gelu(x):
    c = math.sqrt(2.0 / math.pi)
    return 0.5 * x * (1.0 + jnp.tanh(c * (x + 0.044715 * (x * x * x))))


def _sigmoid(x):
    return 0.5 * (jnp.tanh(0.5 * x) + 1.0)


def _resident(shape):
    return pl.BlockSpec(shape, lambda i: (0,) * len(shape), pipeline_mode=pl.Buffered(1))


def _params(semantics, vmem_mib):
    return pltpu.CompilerParams(dimension_semantics=semantics,
                                vmem_limit_bytes=vmem_mib * MIB)


def _inproj_kernel(x_ref, g_ref, w_ref, ggm_ref, qkv_ref, u_ref, vn_ref, gates_ref, hn_ref):
    j = pl.program_id(1)

    @pl.when(j == 0)
    def _():
        hn_ref[...] = (_rms_scale(x_ref[...]) * g_ref[...]).astype(BF16)

    def acc():
        return _dot(hn_ref[...], w_ref[...])

    @pl.when(j < 3)
    def _():
        qkv_ref[...] = acc().astype(BF16)

    @pl.when(j == 3)
    def _():
        u_ref[...] = _gelu(acc()).astype(BF16)

    @pl.when(j == 4)
    def _():
        vn_ref[...] = (_rms_scale(_gelu(acc())) * ggm_ref[...]).astype(BF16)

    @pl.when(j >= 5)
    def _():
        gates_ref[...] = _sigmoid(acc()).astype(BF16)


def _inproj(x2d, g_mix, w_in_bf, g_gm):
    t, d = x2d.shape
    bm, bn = INPROJ_BM, INPROJ_BN
    ncol = w_in_bf.shape[1] // bn
    assert ncol == 9 and t % bm == 0
    out_shape = (
        jax.ShapeDtypeStruct((t, 3 * bn), BF16),
        jax.ShapeDtypeStruct((t, bn), BF16),
        jax.ShapeDtypeStruct((t, bn), BF16),
        jax.ShapeDtypeStruct((t, 4 * bn), BF16),
    )
    return pl.pallas_call(
        _inproj_kernel,
        out_shape=out_shape,
        grid=(t // bm, ncol),
        in_specs=[
            pl.BlockSpec((bm, d), lambda i, j: (i, 0)),
            pl.BlockSpec((1, d), lambda i, j: (0, 0)),
            pl.BlockSpec((d, bn), lambda i, j: (0, j)),
            pl.BlockSpec((1, bn), lambda i, j: (0, 0)),
        ],
        out_specs=(
            pl.BlockSpec((bm, bn), lambda i, j: (i, jnp.minimum(j, 2))),
            pl.BlockSpec((bm, bn), lambda i, j: (i, 0)),
            pl.BlockSpec((bm, bn), lambda i, j: (i, 0)),
            pl.BlockSpec((bm, bn), lambda i, j: (i, jnp.clip(j - 5, 0, 3))),
        ),
        scratch_shapes=[pltpu.VMEM((bm, d), BF16)],
        compiler_params=_params(("parallel", "arbitrary"), 48),
        name="inproj",
    )(x2d, g_mix, w_in_bf, g_gm)


def _sb_attn_kernel(q_ref, k_ref, v_ref, o_ref, *, scale):
    s, _ = q_ref.shape
    qb = ATTN_QB
    kw = ATTN_KEYS
    rowi = lax.broadcasted_iota(I32, (qb, kw), 0)
    coli = lax.broadcasted_iota(I32, (qb, kw), 1)
    r2 = lax.broadcasted_iota(I32, (kw, kw), 0)
    c2 = lax.broadcasted_iota(I32, (kw, kw), 1)
    above = (r2 > c2).astype(BF16)
    above2 = jnp.concatenate([above, above], axis=0)

    def windows(qs, starts, limits, carries):
        starts = [pl.multiple_of(st, qb) for st in starts]
        zs = [_dot_nt(q, k_ref[pl.ds(st, kw), :]) * scale for q, st in zip(qs, starts)]
        valids, log_sigs, log_nots, splits = [], [], [], []
        for z, st, lim in zip(zs, starts, limits):
            valid = coli + st < lim
            lp = jnp.log1p(jnp.exp(-jnp.abs(z)))
            log_not = jnp.where(valid, -jnp.maximum(z, 0.0) - lp, 0.0)
            hi = log_not.astype(BF16)
            mid = (log_not - hi.astype(F32)).astype(BF16)
            valids.append(valid)
            log_sigs.append(jnp.minimum(z, 0.0) - lp)
            log_nots.append(log_not)
            splits.append(jnp.concatenate([hi, mid], axis=1))
        betweens = [_dot(sp, above2) for sp in splits]
        probs = [jnp.where(valid, jnp.exp(ls + bt + carry), 0.0).astype(BF16)
                 for valid, ls, bt, carry in zip(valids, log_sigs, betweens, carries)]
        outs = [_dot(p, v_ref[pl.ds(st, kw), :]) for p, st in zip(probs, starts)]
        new_carries = [carry + bt[:, 0:1] + ln[:, 0:1]
                       for carry, bt, ln in zip(carries, betweens, log_nots)]
        return outs, new_carries

    chains = ATTN_CHAINS
    per_chain = s // qb // chains

    def qgroup(g, _):
        blocks = [c * per_chain + g for c in range(chains)]
        qs = [q_ref[pl.ds(pl.multiple_of(i * qb, qb), qb), :] for i in blocks]
        starts = [jnp.maximum(i - 1, 0) * qb for i in blocks]
        outs, carries = windows(qs, starts, [i * qb + rowi for i in blocks],
                                [jnp.zeros((qb, 1), F32)] * chains)

        def pending(starts, carries):
            need = [jnp.where(st > 0, jnp.max(c), -jnp.inf) for st, c in zip(starts, carries)]
            return functools.reduce(jnp.maximum, need)

        def cond(st):
            return st[3] > -ATTN_SKIP_LOG

        def body(st):
            starts, outs, carries, _ = st
            nxt = [jnp.maximum(k0 - kw, 0) for k0 in starts]
            more, carries = windows(qs, nxt, starts, carries)
            outs = tuple(o + m for o, m in zip(outs, more))
            return tuple(nxt), outs, tuple(carries), pending(nxt, carries)

        _, outs, _, _ = lax.while_loop(
            cond, body, (tuple(starts), tuple(outs), tuple(carries), pending(starts, carries)))
        for i, o in zip(blocks, outs):
            o_ref[pl.ds(pl.multiple_of(i * qb, qb), qb), :] = o.astype(o_ref.dtype)
        return 0

    lax.fori_loop(0, per_chain, qgroup, 0)


def _sb_attn(qkv, batch, seq):
    width = qkv.shape[1] // 3
    dh = width // SB_HEADS
    assert dh == LANES and seq % (ATTN_QB * ATTN_CHAINS) == 0 and seq >= ATTN_KEYS
    qkv3 = qkv.reshape(batch, seq, 3 * width)
    kern = functools.partial(_sb_attn_kernel, scale=1.0 / math.sqrt(dh))
    blk = (None, seq, dh)
    o = pl.pallas_call(
        kern,
        out_shape=jax.ShapeDtypeStruct((batch, seq, width), BF16),
        grid=(batch, SB_HEADS),
        in_specs=[
            pl.BlockSpec(blk, lambda b, h: (b, 0, h)),
            pl.BlockSpec(blk, lambda b, h: (b, 0, SB_HEADS + h)),
            pl.BlockSpec(blk, lambda b, h: (b, 0, 2 * SB_HEADS + h)),
        ],
        out_specs=pl.BlockSpec(blk, lambda b, h: (b, 0, h)),
        compiler_params=_params(("parallel", "parallel"), 48),
        name="sb_attn",
    )(qkv3, qkv3, qkv3)
    return o.reshape(batch * seq, width)


def _merge_kernel(osb_ref, u_ref, vn_ref, gsb_ref, ggm_ref, wsp_ref, bsp_ref, wsb_ref, wgm_ref,
                  out_ref, ogm_ref):
    bm = osb_ref.shape[0]
    c = GM_CHUNK
    row = lax.broadcasted_iota(I32, (c, c), 0)
    col = lax.broadcasted_iota(I32, (c, c), 1)
    causal = row >= col
    for g in range(GM_GROUPS):
        w = jnp.where(causal, wsp_ref[g], 0.0).astype(BF16)
        gs = slice(g * c, (g + 1) * c)
        for ch in range(bm // c):
            rs = slice(ch * c, (ch + 1) * c)
            mixed = _dot(w, vn_ref[rs, gs]) + bsp_ref[g]
            ogm_ref[rs, gs] = (u_ref[rs, gs].astype(F32) * mixed).astype(BF16)
    merged = (gsb_ref[...].astype(F32) * _dot(osb_ref[...], wsb_ref[...])
              + ggm_ref[...].astype(F32) * _dot(ogm_ref[...], wgm_ref[...]))
    out_ref[...] = merged.astype(BF16)


def _merge(o_sb, u, vn, gates, w_spatial, b_bcast, w_sb_bf, w_gm_bf):
    t, width = o_sb.shape
    d = w_sb_bf.shape[1]
    bm = MERGE_BM
    assert t % bm == 0 and bm % GM_CHUNK == 0 and width == GM_GROUPS * GM_CHUNK
    return pl.pallas_call(
        _merge_kernel,
        out_shape=jax.ShapeDtypeStruct((t, d), BF16),
        grid=(t // bm,),
        in_specs=[
            pl.BlockSpec((bm, width), lambda i: (i, 0)),
            pl.BlockSpec((bm, width), lambda i: (i, 0)),
            pl.BlockSpec((bm, width), lambda i: (i, 0)),
            pl.BlockSpec((bm, d), lambda i: (i, 0)),
            pl.BlockSpec((bm, d), lambda i: (i, 1)),
            _resident(w_spatial.shape),
            _resident(b_bcast.shape),
            _resident(w_sb_bf.shape),
            _resident(w_gm_bf.shape),
        ],
        out_specs=pl.BlockSpec((bm, d), lambda i: (i, 0)),
        scratch_shapes=[pltpu.VMEM((bm, width), BF16)],
        compiler_params=_params(("parallel",), 48),
        name="merge",
    )(o_sb, u, vn, gates, gates, w_spatial, b_bcast, w_sb_bf, w_gm_bf)


def _memkv_kernel(mem_ref, g_ref, w_ref, kv_ref):
    mn = (_rms_scale(mem_ref[...]) * g_ref[...]).astype(BF16)
    kv_ref[...] = _dot(mn, w_ref[...]).astype(BF16)


def _memkv(mem2d, g_mem, w_xkv_bf):
    rows, d = mem2d.shape
    n = w_xkv_bf.shape[1]
    bm = 256
    assert rows % bm == 0
    return pl.pallas_call(
        _memkv_kernel,
        out_shape=jax.ShapeDtypeStruct((rows, n), BF16),
        grid=(rows // bm,),
        in_specs=[
            pl.BlockSpec((bm, d), lambda i: (i, 0)),
            pl.BlockSpec((1, d), lambda i: (0, 0)),
            pl.BlockSpec((d, n), lambda i: (0, 0)),
        ],
        out_specs=pl.BlockSpec((bm, n), lambda i: (i, 0)),
        compiler_params=_params(("parallel",), 32),
        name="memkv",
    )(mem2d, g_mem, w_xkv_bf)


def _route_slab(x, g, w_hi_lo, b):
    h = _rms_scale(x) * g
    h_hi = h.astype(BF16)
    h_lo = (h - h_hi.astype(F32)).astype(BF16)
    both = _dot(h_hi, w_hi_lo)
    lg = both[:, :LANES] + both[:, LANES:] + _dot(h_lo, w_hi_lo[:, :LANES]) + b
    lane = lax.broadcasted_iota(I32, lg.shape, 1)
    neg = -jnp.inf

    def first_max(mask):
        v = jnp.max(jnp.where(mask, lg, neg), axis=-1, keepdims=True)
        hit = jnp.logical_and(mask, lg == v)
        return v, jnp.min(jnp.where(hit, lane, LANES), axis=-1, keepdims=True)

    gmask = lane < N_GROUPS
    gmax, gidx = first_max(gmask)
    p_g = 1.0 / jnp.sum(jnp.where(gmask, jnp.exp(lg - gmax), 0.0), axis=-1, keepdims=True)
    lo = N_GROUPS + EXPERTS_PER_GROUP * gidx
    emask = jnp.logical_and(lane >= lo, lane < lo + EXPERTS_PER_GROUP)
    v1, i1 = first_max(emask)
    v2, i2 = first_max(jnp.logical_and(emask, lane != i1))
    t = jnp.exp(v2 - v1)
    w1 = p_g / (1.0 + t)
    w2 = w1 * t
    e1 = (i1 - N_GROUPS).astype(F32)
    e2 = (i2 - N_GROUPS).astype(F32)
    return jnp.where(lane == 0, e1,
                     jnp.where(lane == 1, e2,
                               jnp.where(lane == 2, w1,
                                         jnp.where(lane == 3, w2, 0.0))))


def _cross_kernel(x_ref, m_ref, wout_ref, gc_ref, wxq_ref, kv_ref, wxo_ref,
                  gf_ref, rw_ref, rb_ref, out_ref, route_ref):
    xw = wxq_ref.shape[1]
    dh = xw // X_HEADS
    x1 = x_ref[...] + _dot(m_ref[...], wout_ref[...])
    hc = (_rms_scale(x1) * gc_ref[...]).astype(BF16)
    q = _dot(hc, wxq_ref[...]).astype(BF16)
    inv = 1.0 / math.sqrt(dh)
    heads = []
    for h in range(X_HEADS):
        kh = kv_ref[:, h * dh:(h + 1) * dh]
        vh = kv_ref[:, xw + h * dh: xw + (h + 1) * dh]
        s = _dot_nt(q[:, h * dh:(h + 1) * dh], kh) * inv
        e = jnp.exp(s - jnp.max(s, axis=-1, keepdims=True))
        p = e / jnp.sum(e, axis=-1, keepdims=True)
        heads.append(_dot(p.astype(BF16), vh).astype(BF16))
    o = jnp.concatenate(heads, axis=-1)
    x2 = x1 + _dot(o, wxo_ref[...])
    out_ref[...] = x2
    route_ref[...] = _route_slab(x2, gf_ref[...], rw_ref[...], rb_ref[...])


def _cross(x2d, merged, w_out_bf, g_cross, w_xq_bf, kv, w_xo_bf, g_ffn, w_r_hi_lo, b_r,
           seq, n_mem):
    t, d = x2d.shape
    bm = CROSS_BM
    assert seq % bm == 0
    steps_per_batch = seq // bm
    const2 = lambda i: (0, 0)
    return pl.pallas_call(
        _cross_kernel,
        out_shape=(jax.ShapeDtypeStruct((t, d), F32), jax.ShapeDtypeStruct((t, LANES), F32)),
        grid=(t // bm,),
        in_specs=[
            pl.BlockSpec((bm, d), lambda i: (i, 0)),
            pl.BlockSpec((bm, d), lambda i: (i, 0)),
            _resident(w_out_bf.shape),
            pl.BlockSpec((1, d), const2),
            _resident(w_xq_bf.shape),
            pl.BlockSpec((n_mem, kv.shape[1]), lambda i: (i // steps_per_batch, 0)),
            _resident(w_xo_bf.shape),
            pl.BlockSpec((1, d), const2),
            _resident(w_r_hi_lo.shape),
            pl.BlockSpec((1, LANES), const2),
        ],
        out_specs=(pl.BlockSpec((bm, d), lambda i: (i, 0)),
                   pl.BlockSpec((bm, LANES), lambda i: (i, 0))),
        compiler_params=_params(("parallel",), 52),
        name="cross",
    )(x2d, merged, w_out_bf, g_cross, w_xq_bf, kv, w_xo_bf, g_ffn, w_r_hi_lo, b_r)


def _rank_kernel(e_ref, rank_ref, cnt_ref, carry_ref):
    @pl.when(pl.program_id(0) == 0)
    def _():
        carry_ref[...] = jnp.zeros_like(carry_ref)

    ex = lax.broadcasted_iota(I32, (N_EXPERTS, LANES), 0)
    r = lax.broadcasted_iota(I32, (LANES, LANES), 0)
    c = lax.broadcasted_iota(I32, (LANES, LANES), 1)
    before = (r < c).astype(BF16)
    carry = carry_ref[...]
    for i in range(e_ref.shape[0]):
        hit = ex == e_ref[i:i + 1, :]
        onehot = jnp.where(hit, 1.0, 0.0)
        prefix = _dot(onehot.astype(BF16), before) + carry
        rank_ref[i:i + 1, :] = jnp.sum(jnp.where(hit, prefix, 0.0), axis=0, keepdims=True).astype(I32)
        carry = carry + jnp.sum(onehot, axis=1, keepdims=True)
    carry_ref[...] = carry
    cnt_ref[...] = carry


def _rank(e2d):
    rows = e2d.shape[0]
    rr = RANK_ROWS
    assert rows % rr == 0
    return pl.pallas_call(
        _rank_kernel,
        out_shape=(jax.ShapeDtypeStruct((rows, LANES), I32),
                   jax.ShapeDtypeStruct((N_EXPERTS, LANES), F32)),
        grid=(rows // rr,),
        in_specs=[pl.BlockSpec((rr, LANES), lambda i: (i, 0))],
        out_specs=(pl.BlockSpec((rr, LANES), lambda i: (i, 0)),
                   pl.BlockSpec((N_EXPERTS, LANES), lambda i: (0, 0))),
        scratch_shapes=[pltpu.VMEM((N_EXPERTS, LANES), F32)],
        compiler_params=_params(("arbitrary",), 16),
        name="rank",
    )(e2d)


def _moe_kernel(be_ref, nu_ref, src0_ref, src1_ref, src_next_ref, dst_prev_ref, x_hbm, g_ref,
                wg_ref, wu_ref, wd_ref, y_hbm, xbuf, ybuf, xn_ref, gsem, ssem):
    i = pl.program_id(0)
    n_used = nu_ref[0]
    bm = MOE_BM

    def gather(idx_ref, r, s):
        return pltpu.make_async_copy(x_hbm.at[pl.ds(idx_ref[0, 0, r], 1)],
                                     xbuf.at[s, pl.ds(r, 1)], gsem.at[s])

    def scatter(r, s):
        return pltpu.make_async_copy(ybuf.at[s, pl.ds(r, 1)],
                                     y_hbm.at[pl.ds(dst_prev_ref[0, 0, r], 1)], ssem.at[s])

    def wait_rows(make):
        for _ in range(bm):
            make().wait()

    def start_scatters(s):
        for r in range(bm):
            scatter(r, s).start(priority=r % 2)

    @pl.when(i == 0)
    def _():
        ybuf[...] = jnp.zeros_like(ybuf)
        for r in range(bm):
            gather(src0_ref, r, 0).start()
        for r in range(bm):
            gather(src1_ref, r, 1).start()

    def step(slot):
        ahead = (slot + 2) % MOE_RING
        done = (slot + 1) % MOE_RING

        @pl.when(i <= n_used + 1)
        def _():
            wait_rows(lambda: gather(src0_ref, 0, slot))

        @pl.when(i < n_used)
        def _():
            xn_ref[...] = (_rms_scale(xbuf[slot]) * g_ref[...]).astype(BF16)
            for r in range(bm):
                gather(src_next_ref, r, ahead).start(priority=r % 2)
            start_scatters(ahead)
            xn = xn_ref[...]
            a = _dot(xn, wg_ref[...])
            b = _dot(xn, wu_ref[...])
            h = (a * _sigmoid(a) * b).astype(BF16)
            ybuf[slot] = _dot(h, wd_ref[...])

        @pl.when(i == n_used)
        def _():
            start_scatters(ahead)

        @pl.when(jnp.logical_and(i >= 1, i <= n_used + 1))
        def _():
            wait_rows(lambda: scatter(0, done))

    phase = lax.rem(i, MOE_RING)
    for slot in range(MOE_RING):
        pl.when(phase == slot)(functools.partial(step, slot))


def _moe(x2d, g_ffn, wg_bf, wu_bf, wd_bf, blk_e, n_used, src3, dst3, n_blocks, n_out_rows):
    d = x2d.shape[1]
    de = wg_bf.shape[2]
    bm = MOE_BM
    last = lambda nu: nu[0] - 1
    smem_blk = lambda imap: pl.BlockSpec((1, 1, bm), imap, memory_space=pltpu.SMEM)
    expert = lambda i, be, nu: (be[jnp.minimum(i, last(nu))], 0, 0)
    return pl.pallas_call(
        _moe_kernel,
        out_shape=jax.ShapeDtypeStruct((n_out_rows, d), F32),
        grid_spec=pltpu.PrefetchScalarGridSpec(
            num_scalar_prefetch=2,
            grid=(n_blocks + 2,),
            in_specs=[
                smem_blk(lambda i, be, nu: (0, 0, 0)),
                smem_blk(lambda i, be, nu: (jnp.minimum(1, last(nu)), 0, 0)),
                smem_blk(lambda i, be, nu: (jnp.minimum(i + 2, last(nu)), 0, 0)),
                smem_blk(lambda i, be, nu: (jnp.minimum(i, nu[0]), 0, 0)),
                pl.BlockSpec(memory_space=pl.ANY),
                pl.BlockSpec((1, d), lambda i, be, nu: (0, 0)),
                pl.BlockSpec((None, d, de), expert),
                pl.BlockSpec((None, d, de), expert),
                pl.BlockSpec((None, de, d), expert),
            ],
            out_specs=pl.BlockSpec(memory_space=pl.ANY),
            scratch_shapes=[
                pltpu.VMEM((MOE_RING, bm, d), F32),
                pltpu.VMEM((MOE_RING, bm, d), F32),
                pltpu.VMEM((bm, d), BF16),
                pltpu.SemaphoreType.DMA((MOE_RING,)),
                pltpu.SemaphoreType.DMA((MOE_RING,)),
            ],
        ),
        compiler_params=_params(("arbitrary",), 56),
        name="moe",
    )(blk_e, n_used, src3, src3, src3, dst3, x2d, g_ffn, wg_bf, wu_bf, wd_bf)


def _combine_kernel(x_ref, route_ref, g_ref, y1_ref, y2_ref, out_ref):
    w1 = route_ref[:, 2:3]
    w2 = route_ref[:, 3:4]
    x3 = x_ref[...] + (w1 * y1_ref[...] + w2 * y2_ref[...])
    out_ref[...] = _rms_scale(x3) * g_ref[...]


def _combine(x2d, route, g_final, y):
    t, d = x2d.shape
    tb = COMBINE_BM
    steps = t // tb
    return pl.pallas_call(
        _combine_kernel,
        out_shape=jax.ShapeDtypeStruct((t, d), F32),
        grid=(steps,),
        in_specs=[
            pl.BlockSpec((tb, d), lambda i: (i, 0)),
            pl.BlockSpec((tb, LANES), lambda i: (i, 0)),
            pl.BlockSpec((1, d), lambda i: (0, 0)),
            pl.BlockSpec((tb, d), lambda i: (i, 0)),
            pl.BlockSpec((tb, d), lambda i: (steps + i, 0)),
        ],
        out_specs=pl.BlockSpec((tb, d), lambda i: (i, 0)),
        compiler_params=_params(("parallel",), 40),
        name="combine",
    )(x2d, route, g_final, y, y)


def _layer(x, mem, g_mix, w_in, g_gm, w_spatial, b_spatial, w_branch_sb, w_branch_gm, w_out,
           g_cross, g_mem, w_xq, w_xkv, w_xo, g_ffn, w_rg, b_rg, w_re, b_re,
           w_e_gate, w_e_up, w_e_down, g_final):
    batch, seq, d = x.shape
    n_mem = mem.shape[1]
    t = batch * seq
    row = lambda v: v.reshape(1, -1)
    x2d = x.reshape(t, d)

    qkv, u, vn, gates = _inproj(x2d, row(g_mix), w_in.astype(BF16), row(g_gm))
    o_sb = _sb_attn(qkv, batch, seq)
    b_bcast = jnp.broadcast_to(b_spatial[:, :, None], b_spatial.shape + (GM_CHUNK,))
    merged = _merge(o_sb, u, vn, gates, w_spatial, b_bcast,
                    w_branch_sb.astype(BF16), w_branch_gm.astype(BF16))
    kv = _memkv(mem.reshape(batch * n_mem, d), row(g_mem), w_xkv.astype(BF16))
    pad = LANES - N_GROUPS - N_EXPERTS
    w_r = jnp.concatenate([w_rg, w_re, jnp.zeros((d, pad), F32)], axis=1)
    b_r = jnp.concatenate([b_rg, b_re, jnp.zeros((pad,), F32)]).reshape(1, LANES)
    w_r_hi = w_r.astype(BF16)
    w_r_lo = (w_r - w_r_hi.astype(F32)).astype(BF16)
    x2, route = _cross(x2d, merged, w_out.astype(BF16), row(g_cross), w_xq.astype(BF16), kv,
                       w_xo.astype(BF16), row(g_ffn), jnp.concatenate([w_r_hi, w_r_lo], axis=1),
                       b_r, seq, n_mem)

    n = t * TOP_K
    e_flat = route[:, :TOP_K].astype(I32).reshape(n)
    rank2d, cnt = _rank(e_flat.reshape(n // LANES, LANES))
    counts = cnt[:, 0].astype(I32)
    padded = ((counts + MOE_BM - 1) // MOE_BM) * MOE_BM
    pends = jnp.cumsum(padded)
    pstarts = pends - padded
    dest = pstarts[e_flat] + rank2d.reshape(n)
    n_blocks = n // MOE_BM + N_EXPERTS
    blk_start = jnp.arange(n_blocks, dtype=I32) * MOE_BM
    blk_e = jnp.minimum(jnp.sum(pends[None, :] <= blk_start[:, None], axis=1),
                        N_EXPERTS - 1).astype(I32)
    n_used = (pends[-1:] // MOE_BM).astype(I32)
    inv = jnp.full((n_blocks * MOE_BM,), -1, I32).at[dest].set(
        jnp.arange(n, dtype=I32), unique_indices=True, mode="promise_in_bounds")
    inv = inv.reshape(n_blocks, MOE_BM)
    tok = jnp.maximum(inv, 0) // TOP_K
    src3 = tok.reshape(n_blocks, 1, MOE_BM)
    spare = n + (jnp.arange(n_blocks, dtype=I32)[:, None] % 2) * MOE_BM + jnp.arange(MOE_BM, dtype=I32)
    dst = jnp.where(inv >= 0, (inv % TOP_K) * t + tok, spare)
    first = (n + MOE_BM + jnp.arange(MOE_BM, dtype=I32))[None]
    dst3 = jnp.concatenate([first, dst], axis=0).reshape(n_blocks + 1, 1, MOE_BM)

    y = _moe(x2, row(g_ffn), w_e_gate.astype(BF16), w_e_up.astype(BF16), w_e_down.astype(BF16),
             blk_e, n_used, src3, dst3, n_blocks, n + 2 * MOE_BM)
    out = _combine(x2, route, row(g_final), y)
    return out.reshape(batch, seq, d)


def kernel(x, mem, g_mix, w_in, g_gm, w_spatial, b_spatial, w_branch_sb, w_branch_gm, w_out,
           g_cross, g_mem, w_xq, w_xkv, w_xo, g_ffn, w_rg, b_rg, w_re, b_re,
           w_e_gate, w_e_up, w_e_down, g_final):
    assert w_in.shape[0] == 1, "single layer"
    return _layer(x, mem, g_mix[0], w_in[0], g_gm[0], w_spatial[0], b_spatial[0],
                  w_branch_sb[0], w_branch_gm[0], w_out[0], g_cross[0], g_mem[0], w_xq[0],
                  w_xkv[0], w_xo[0], g_ffn[0], w_rg[0], b_rg[0], w_re[0], b_re[0],
                  w_e_gate[0], w_e_up[0], w_e_down[0], g_final)
```

```python
import functools
import math

import jax
import jax.numpy as jnp
from jax import lax
from jax.experimental import pallas as pl
from jax.experimental.pallas import tpu as pltpu

F32 = jnp.float32
BF16 = jnp.bfloat16
I32 = jnp.int32

EPS = 1e-6
LANES = 128
MIB = 1024 * 1024

SB_HEADS = 8
GM_GROUPS = 8
GM_CHUNK = 128
X_HEADS = 4
N_GROUPS = 4
EXPERTS_PER_GROUP = 8
N_EXPERTS = N_GROUPS * EXPERTS_PER_GROUP
TOP_K = 2

INPROJ_BM = 512
INPROJ_BN = 1024
ATTN_QB = 128
ATTN_KEYS = 256
ATTN_CHAINS = 4
MERGE_BM = 512
CROSS_BM = 512
RANK_ROWS = 8
MOE_BM = 256
MOE_RING = 3
COMBINE_BM = 256
ATTN_SKIP_LOG = 88.0


def _dot(a, b):
    return jnp.dot(a, b, preferred_element_type=F32)


def _dot_nt(a, b):
    return lax.dot_general(a, b, (((1,), (1,)), ((), ())), preferred_element_type=F32)


def _pack_halves(y):
    half = y.shape[1] // 2
    hi = pltpu.bitcast(y[:, :half].astype(BF16).astype(F32), jnp.uint32)
    lo = pltpu.bitcast(y[:, half:].astype(BF16).astype(F32), jnp.uint32)
    return hi | lax.shift_right_logical(lo, jnp.uint32(16))


def _unpack_halves(w):
    hi = pltpu.bitcast(w & jnp.uint32(0xFFFF0000), F32)
    lo = pltpu.bitcast(lax.shift_left(w, jnp.uint32(16)), F32)
    return jnp.concatenate([hi, lo], axis=1)


def _rms_scale(x):
    return x * lax.rsqrt(jnp.mean(x * x, axis=-1, keepdims=True) + EPS)


def _gelu(x):
    c = math.sqrt(2.0 / math.pi)
    return 0.5 * x * (1.0 + jnp.tanh(c * (x + 0.044715 * (x * x * x))))


def _sigmoid(x):
    return 0.5 * (jnp.tanh(0.5 * x) + 1.0)


def _resident(shape):
    return pl.BlockSpec(shape, lambda i: (0,) * len(shape), pipeline_mode=pl.Buffered(1))


def _params(semantics, vmem_mib):
    return pltpu.CompilerParams(dimension_semantics=semantics,
                                vmem_limit_bytes=vmem_mib * MIB)


def _inproj_kernel(x_ref, g_ref, w_ref, ggm_ref, qkv_ref, u_ref, vn_ref, gates_ref, hn_ref):
    j = pl.program_id(1)

    @pl.when(j == 0)
    def _():
        hn_ref[...] = (_rms_scale(x_ref[...]) * g_ref[...]).astype(BF16)

    def acc():
        return _dot(hn_ref[...], w_ref[...])

    @pl.when(j < 3)
    def _():
        qkv_ref[...] = acc().astype(BF16)

    @pl.when(j == 3)
    def _():
        u_ref[...] = _gelu(acc()).astype(BF16)

    @pl.when(j == 4)
    def _():
        vn_ref[...] = (_rms_scale(_gelu(acc())) * ggm_ref[...]).astype(BF16)

    @pl.when(j >= 5)
    def _():
        gates_ref[...] = _sigmoid(acc()).astype(BF16)


def _inproj(x2d, g_mix, w_in_bf, g_gm):
    t, d = x2d.shape
    bm, bn = INPROJ_BM, INPROJ_BN
    ncol = w_in_bf.shape[1] // bn
    assert ncol == 9 and t % bm == 0
    out_shape = (
        jax.ShapeDtypeStruct((t, 3 * bn), BF16),
        jax.ShapeDtypeStruct((t, bn), BF16),
        jax.ShapeDtypeStruct((t, bn), BF16),
        jax.ShapeDtypeStruct((t, 4 * bn), BF16),
    )
    return pl.pallas_call(
        _inproj_kernel,
        out_shape=out_shape,
        grid=(t // bm, ncol),
        in_specs=[
            pl.BlockSpec((bm, d), lambda i, j: (i, 0)),
            pl.BlockSpec((1, d), lambda i, j: (0, 0)),
            pl.BlockSpec((d, bn), lambda i, j: (0, j)),
            pl.BlockSpec((1, bn), lambda i, j: (0, 0)),
        ],
        out_specs=(
            pl.BlockSpec((bm, bn), lambda i, j: (i, jnp.minimum(j, 2))),
            pl.BlockSpec((bm, bn), lambda i, j: (i, 0)),
            pl.BlockSpec((bm, bn), lambda i, j: (i, 0)),
            pl.BlockSpec((bm, bn), lambda i, j: (i, jnp.clip(j - 5, 0, 3))),
        ),
        scratch_shapes=[pltpu.VMEM((bm, d), BF16)],
        compiler_params=_params(("parallel", "arbitrary"), 48),
        name="inproj",
    )(x2d, g_mix, w_in_bf, g_gm)


def _sb_attn_kernel(q_ref, k_ref, v_ref, o_ref, *, scale):
    s, _ = q_ref.shape
    qb = ATTN_QB
    kw = ATTN_KEYS
    rowi = lax.broadcasted_iota(I32, (qb, kw), 0)
    coli = lax.broadcasted_iota(I32, (qb, kw), 1)
    r2 = lax.broadcasted_iota(I32, (kw, kw), 0)
    c2 = lax.broadcasted_iota(I32, (kw, kw), 1)
    above = (r2 > c2).astype(BF16)
    above2 = jnp.concatenate([above, above], axis=0)

    def windows(qs, starts, limits, carries):
        starts = [pl.multiple_of(st, qb) for st in starts]
        zs = [_dot_nt(q, k_ref[pl.ds(st, kw), :]) * scale for q, st in zip(qs, starts)]
        valids, log_sigs, log_nots, splits = [], [], [], []
        for z, st, lim in zip(zs, starts, limits):
            valid = coli + st < lim
            lp = jnp.log1p(jnp.exp(-jnp.abs(z)))
            log_not = jnp.where(valid, -jnp.maximum(z, 0.0) - lp, 0.0)
            hi = log_not.astype(BF16)
            mid = (log_not - hi.astype(F32)).astype(BF16)
            valids.append(valid)
            log_sigs.append(jnp.minimum(z, 0.0) - lp)
            log_nots.append(log_not)
            splits.append(jnp.concatenate([hi, mid], axis=1))
        betweens = [_dot(sp, above2) for sp in splits]
        probs = [jnp.where(valid, jnp.exp(ls + bt + carry), 0.0).astype(BF16)
                 for valid, ls, bt, carry in zip(valids, log_sigs, betweens, carries)]
        outs = [_dot(p, v_ref[pl.ds(st, kw), :]) for p, st in zip(probs, starts)]
        new_carries = [carry + bt[:, 0:1] + ln[:, 0:1]
                       for carry, bt, ln in zip(carries, betweens, log_nots)]
        return outs, new_carries

    chains = ATTN_CHAINS
    per_chain = s // qb // chains

    def qgroup(g, _):
        blocks = [c * per_chain + g for c in range(chains)]
        qs = [q_ref[pl.ds(pl.multiple_of(i * qb, qb), qb), :] for i in blocks]
        starts = [jnp.maximum(i - 1, 0) * qb for i in blocks]
        outs, carries = windows(qs, starts, [i * qb + rowi for i in blocks],
                                [jnp.zeros((qb, 1), F32)] * chains)

        def pending(starts, carries):
            need = [jnp.where(st > 0, jnp.max(c), -jnp.inf) for st, c in zip(starts, carries)]
            return functools.reduce(jnp.maximum, need)

        def cond(st):
            return st[3] > -ATTN_SKIP_LOG

        def body(st):
            starts, outs, carries, _ = st
            nxt = [jnp.maximum(k0 - kw, 0) for k0 in starts]
            more, carries = windows(qs, nxt, starts, carries)
            outs = tuple(o + m for o, m in zip(outs, more))
            return tuple(nxt), outs, tuple(carries), pending(nxt, carries)

        _, outs, _, _ = lax.while_loop(
            cond, body, (tuple(starts), tuple(outs), tuple(carries), pending(starts, carries)))
        for i, o in zip(blocks, outs):
            o_ref[pl.ds(pl.multiple_of(i * qb, qb), qb), :] = o.astype(o_ref.dtype)
        return 0

    lax.fori_loop(0, per_chain, qgroup, 0)


def _sb_attn(qkv, batch, seq):
    width = qkv.shape[1] // 3
    dh = width // SB_HEADS
    assert dh == LANES and seq % (ATTN_QB * ATTN_CHAINS) == 0 and seq >= ATTN_KEYS
    qkv3 = qkv.reshape(batch, seq, 3 * width)
    kern = functools.partial(_sb_attn_kernel, scale=1.0 / math.sqrt(dh))
    blk = (None, seq, dh)
    o = pl.pallas_call(
        kern,
        out_shape=jax.ShapeDtypeStruct((batch, seq, width), BF16),
        grid=(batch, SB_HEADS),
        in_specs=[
            pl.BlockSpec(blk, lambda b, h: (b, 0, h)),
            pl.BlockSpec(blk, lambda b, h: (b, 0, SB_HEADS + h)),
            pl.BlockSpec(blk, lambda b, h: (b, 0, 2 * SB_HEADS + h)),
        ],
        out_specs=pl.BlockSpec(blk, lambda b, h: (b, 0, h)),
        compiler_params=_params(("parallel", "parallel"), 48),
        name="sb_attn",
    )(qkv3, qkv3, qkv3)
    return o.reshape(batch * seq, width)


def _merge_kernel(osb_ref, u_ref, vn_ref, gsb_ref, ggm_ref, wsp_ref, bsp_ref, wsb_ref, wgm_ref,
                  out_ref, ogm_ref):
    bm = osb_ref.shape[0]
    c = GM_CHUNK
    row = lax.broadcasted_iota(I32, (c, c), 0)
    col = lax.broadcasted_iota(I32, (c, c), 1)
    causal = row >= col
    for g in range(GM_GROUPS):
        w = jnp.where(causal, wsp_ref[g], 0.0).astype(BF16)
        gs = slice(g * c, (g + 1) * c)
        for ch in range(bm // c):
            rs = slice(ch * c, (ch + 1) * c)
            mixed = _dot(w, vn_ref[rs, gs]) + bsp_ref[g]
            ogm_ref[rs, gs] = (u_ref[rs, gs].astype(F32) * mixed).astype(BF16)
    merged = (gsb_ref[...].astype(F32) * _dot(osb_ref[...], wsb_ref[...])
              + ggm_ref[...].astype(F32) * _dot(ogm_ref[...], wgm_ref[...]))
    out_ref[...] = merged.astype(BF16)


def _merge(o_sb, u, vn, gates, w_spatial, b_bcast, w_sb_bf, w_gm_bf):
    t, width = o_sb.shape
    d = w_sb_bf.shape[1]
    bm = MERGE_BM
    assert t % bm == 0 and bm % GM_CHUNK == 0 and width == GM_GROUPS * GM_CHUNK
    return pl.pallas_call(
        _merge_kernel,
        out_shape=jax.ShapeDtypeStruct((t, d), BF16),
        grid=(t // bm,),
        in_specs=[
            pl.BlockSpec((bm, width), lambda i: (i, 0)),
            pl.BlockSpec((bm, width), lambda i: (i, 0)),
            pl.BlockSpec((bm, width), lambda i: (i, 0)),
            pl.BlockSpec((bm, d), lambda i: (i, 0)),
            pl.BlockSpec((bm, d), lambda i: (i, 1)),
            _resident(w_spatial.shape),
            _resident(b_bcast.shape),
            _resident(w_sb_bf.shape),
            _resident(w_gm_bf.shape),
        ],
        out_specs=pl.BlockSpec((bm, d), lambda i: (i, 0)),
        scratch_shapes=[pltpu.VMEM((bm, width), BF16)],
        compiler_params=_params(("parallel",), 48),
        name="merge",
    )(o_sb, u, vn, gates, gates, w_spatial, b_bcast, w_sb_bf, w_gm_bf)


def _memkv_kernel(mem_ref, g_ref, w_ref, kv_ref):
    mn = (_rms_scale(mem_ref[...]) * g_ref[...]).astype(BF16)
    kv_ref[...] = _dot(mn, w_ref[...]).astype(BF16)


def _memkv(mem2d, g_mem, w_xkv_bf):
    rows, d = mem2d.shape
    n = w_xkv_bf.shape[1]
    bm = 256
    assert rows % bm == 0
    return pl.pallas_call(
        _memkv_kernel,
        out_shape=jax.ShapeDtypeStruct((rows, n), BF16),
        grid=(rows // bm,),
        in_specs=[
            pl.BlockSpec((bm, d), lambda i: (i, 0)),
            pl.BlockSpec((1, d), lambda i: (0, 0)),
            pl.BlockSpec((d, n), lambda i: (0, 0)),
        ],
        out_specs=pl.BlockSpec((bm, n), lambda i: (i, 0)),
        compiler_params=_params(("parallel",), 32),
        name="memkv",
    )(mem2d, g_mem, w_xkv_bf)


def _route_slab(x, g, w_hi_lo, b):
    h = _rms_scale(x) * g
    h_hi = h.astype(BF16)
    h_lo = (h - h_hi.astype(F32)).astype(BF16)
    both = _dot(h_hi, w_hi_lo)
    lg = both[:, :LANES] + both[:, LANES:] + _dot(h_lo, w_hi_lo[:, :LANES]) + b
    lane = lax.broadcasted_iota(I32, lg.shape, 1)
    neg = -jnp.inf

    def first_max(mask):
        v = jnp.max(jnp.where(mask, lg, neg), axis=-1, keepdims=True)
        hit = jnp.logical_and(mask, lg == v)
        return v, jnp.min(jnp.where(hit, lane, LANES), axis=-1, keepdims=True)

    gmask = lane < N_GROUPS
    gmax, gidx = first_max(gmask)
    p_g = 1.0 / jnp.sum(jnp.where(gmask, jnp.exp(lg - gmax), 0.0), axis=-1, keepdims=True)
    lo = N_GROUPS + EXPERTS_PER_GROUP * gidx
    emask = jnp.logical_and(lane >= lo, lane < lo + EXPERTS_PER_GROUP)
    v1, i1 = first_max(emask)
    v2, i2 = first_max(jnp.logical_and(emask, lane != i1))
    t = jnp.exp(v2 - v1)
    w1 = p_g / (1.0 + t)
    w2 = w1 * t
    e1 = (i1 - N_GROUPS).astype(F32)
    e2 = (i2 - N_GROUPS).astype(F32)
    return jnp.where(lane == 0, e1,
                     jnp.where(lane == 1, e2,
                               jnp.where(lane == 2, w1,
                                         jnp.where(lane == 3, w2, 0.0))))


def _cross_kernel(x_ref, m_ref, wout_ref, gc_ref, wxq_ref, kv_ref, wxo_ref,
                  gf_ref, rw_ref, rb_ref, out_ref, route_ref):
    xw = wxq_ref.shape[1]
    dh = xw // X_HEADS
    x1 = x_ref[...] + _dot(m_ref[...], wout_ref[...])
    hc = (_rms_scale(x1) * gc_ref[...]).astype(BF16)
    q = _dot(hc, wxq_ref[...]).astype(BF16)
    inv = 1.0 / math.sqrt(dh)
    heads = []
    for h in range(X_HEADS):
        kh = kv_ref[:, h * dh:(h + 1) * dh]
        vh = kv_ref[:, xw + h * dh: xw + (h + 1) * dh]
        s = _dot_nt(q[:, h * dh:(h + 1) * dh], kh) * inv
        e = jnp.exp(s - jnp.max(s, axis=-1, keepdims=True))
        p = e / jnp.sum(e, axis=-1, keepdims=True)
        heads.append(_dot(p.astype(BF16), vh).astype(BF16))
    o = jnp.concatenate(heads, axis=-1)
    x2 = x1 + _dot(o, wxo_ref[...])
    out_ref[...] = x2
    route_ref[...] = _route_slab(x2, gf_ref[...], rw_ref[...], rb_ref[...])


def _cross(x2d, merged, w_out_bf, g_cross, w_xq_bf, kv, w_xo_bf, g_ffn, w_r_hi_lo, b_r,
           seq, n_mem):
    t, d = x2d.shape
    bm = CROSS_BM
    assert seq % bm == 0
    steps_per_batch = seq // bm
    const2 = lambda i: (0, 0)
    return pl.pallas_call(
        _cross_kernel,
        out_shape=(jax.ShapeDtypeStruct((t, d), F32), jax.ShapeDtypeStruct((t, LANES), F32)),
        grid=(t // bm,),
        in_specs=[
            pl.BlockSpec((bm, d), lambda i: (i, 0)),
            pl.BlockSpec((bm, d), lambda i: (i, 0)),
            _resident(w_out_bf.shape),
            pl.BlockSpec((1, d), const2),
            _resident(w_xq_bf.shape),
            pl.BlockSpec((n_mem, kv.shape[1]), lambda i: (i // steps_per_batch, 0)),
            _resident(w_xo_bf.shape),
            pl.BlockSpec((1, d), const2),
            _resident(w_r_hi_lo.shape),
            pl.BlockSpec((1, LANES), const2),
        ],
        out_specs=(pl.BlockSpec((bm, d), lambda i: (i, 0)),
                   pl.BlockSpec((bm, LANES), lambda i: (i, 0))),
        compiler_params=_params(("parallel",), 52),
        name="cross",
    )(x2d, merged, w_out_bf, g_cross, w_xq_bf, kv, w_xo_bf, g_ffn, w_r_hi_lo, b_r)


def _rank_kernel(e_ref, rank_ref, cnt_ref, carry_ref):
    @pl.when(pl.program_id(0) == 0)
    def _():
        carry_ref[...] = jnp.zeros_like(carry_ref)

    ex = lax.broadcasted_iota(I32, (N_EXPERTS, LANES), 0)
    r = lax.broadcasted_iota(I32, (LANES, LANES), 0)
    c = lax.broadcasted_iota(I32, (LANES, LANES), 1)
    before = (r < c).astype(BF16)
    carry = carry_ref[...]
    for i in range(e_ref.shape[0]):
        hit = ex == e_ref[i:i + 1, :]
        onehot = jnp.where(hit, 1.0, 0.0)
        prefix = _dot(onehot.astype(BF16), before) + carry
        rank_ref[i:i + 1, :] = jnp.sum(jnp.where(hit, prefix, 0.0), axis=0, keepdims=True).astype(I32)
        carry = carry + jnp.sum(onehot, axis=1, keepdims=True)
    carry_ref[...] = carry
    cnt_ref[...] = carry


def _rank(e2d):
    rows = e2d.shape[0]
    rr = RANK_ROWS
    assert rows % rr == 0
    return pl.pallas_call(
        _rank_kernel,
        out_shape=(jax.ShapeDtypeStruct((rows, LANES), I32),
                   jax.ShapeDtypeStruct((N_EXPERTS, LANES), F32)),
        grid=(rows // rr,),
        in_specs=[pl.BlockSpec((rr, LANES), lambda i: (i, 0))],
        out_specs=(pl.BlockSpec((rr, LANES), lambda i: (i, 0)),
                   pl.BlockSpec((N_EXPERTS, LANES), lambda i: (0, 0))),
        scratch_shapes=[pltpu.VMEM((N_EXPERTS, LANES), F32)],
        compiler_params=_params(("arbitrary",), 16),
        name="rank",
    )(e2d)


def _moe_kernel(be_ref, nu_ref, src0_ref, src1_ref, src_next_ref, dst_prev_ref, x_hbm, g_ref,
                wg_ref, wu_ref, wd_ref, y_hbm, xbuf, ybuf, xn_ref, gsem, ssem):
    i = pl.program_id(0)
    n_used = nu_ref[0]
    bm = MOE_BM

    def gather(idx_ref, r, s):
        return pltpu.make_async_copy(x_hbm.at[pl.ds(idx_ref[0, 0, r], 1)],
                                     xbuf.at[s, pl.ds(r, 1)], gsem.at[s])

    def scatter(r, s):
        return pltpu.make_async_copy(ybuf.at[s, pl.ds(r, 1)],
                                     y_hbm.at[pl.ds(dst_prev_ref[0, 0, r], 1)], ssem.at[s])

    def wait_rows(make):
        for _ in range(bm):
            make().wait()

    def start_scatters(s):
        for r in range(bm):
            scatter(r, s).start(priority=r % 2)

    @pl.when(i == 0)
    def _():
        ybuf[...] = jnp.zeros_like(ybuf)
        for r in range(bm):
            gather(src0_ref, r, 0).start()
        for r in range(bm):
            gather(src1_ref, r, 1).start()

    def step(slot):
        ahead = (slot + 2) % MOE_RING
        done = (slot + 1) % MOE_RING

        @pl.when(i <= n_used + 1)
        def _():
            wait_rows(lambda: gather(src0_ref, 0, slot))

        @pl.when(i < n_used)
        def _():
            xn_ref[...] = (_rms_scale(xbuf[slot]) * g_ref[...]).astype(BF16)
            for r in range(bm):
                gather(src_next_ref, r, ahead).start(priority=r % 2)
            start_scatters(ahead)
            xn = xn_ref[...]
            a = _dot(xn, wg_ref[...])
            b = _dot(xn, wu_ref[...])
            h = (a * _sigmoid(a) * b).astype(BF16)
            ybuf[slot] = _pack_halves(_dot(h, wd_ref[...]))

        @pl.when(i == n_used)
        def _():
            start_scatters(ahead)

        @pl.when(jnp.logical_and(i >= 1, i <= n_used + 1))
        def _():
            wait_rows(lambda: scatter(0, done))

    phase = lax.rem(i, MOE_RING)
    for slot in range(MOE_RING):
        pl.when(phase == slot)(functools.partial(step, slot))


def _moe(x2d, g_ffn, wg_bf, wu_bf, wd_bf, blk_e, n_used, src3, dst3, n_blocks, n_out_rows):
    d = x2d.shape[1]
    de = wg_bf.shape[2]
    bm = MOE_BM
    last = lambda nu: nu[0] - 1
    smem_blk = lambda imap: pl.BlockSpec((1, 1, bm), imap, memory_space=pltpu.SMEM)
    expert = lambda i, be, nu: (be[jnp.minimum(i, last(nu))], 0, 0)
    return pl.pallas_call(
        _moe_kernel,
        out_shape=jax.ShapeDtypeStruct((n_out_rows, d // 2), jnp.uint32),
        grid_spec=pltpu.PrefetchScalarGridSpec(
            num_scalar_prefetch=2,
            grid=(n_blocks + 2,),
            in_specs=[
                smem_blk(lambda i, be, nu: (0, 0, 0)),
                smem_blk(lambda i, be, nu: (jnp.minimum(1, last(nu)), 0, 0)),
                smem_blk(lambda i, be, nu: (jnp.minimum(i + 2, last(nu)), 0, 0)),
                smem_blk(lambda i, be, nu: (jnp.minimum(i, nu[0]), 0, 0)),
                pl.BlockSpec(memory_space=pl.ANY),
                pl.BlockSpec((1, d), lambda i, be, nu: (0, 0)),
                pl.BlockSpec((None, d, de), expert),
                pl.BlockSpec((None, d, de), expert),
                pl.BlockSpec((None, de, d), expert),
            ],
            out_specs=pl.BlockSpec(memory_space=pl.ANY),
            scratch_shapes=[
                pltpu.VMEM((MOE_RING, bm, d), F32),
                pltpu.VMEM((MOE_RING, bm, d // 2), jnp.uint32),
                pltpu.VMEM((bm, d), BF16),
                pltpu.SemaphoreType.DMA((MOE_RING,)),
                pltpu.SemaphoreType.DMA((MOE_RING,)),
            ],
        ),
        compiler_params=_params(("arbitrary",), 56),
        name="moe",
    )(blk_e, n_used, src3, src3, src3, dst3, x2d, g_ffn, wg_bf, wu_bf, wd_bf)


def _combine_kernel(x_ref, route_ref, g_ref, y1_ref, y2_ref, out_ref):
    w1 = route_ref[:, 2:3]
    w2 = route_ref[:, 3:4]
    x3 = x_ref[...] + (w1 * _unpack_halves(y1_ref[...]) + w2 * _unpack_halves(y2_ref[...]))
    out_ref[...] = _rms_scale(x3) * g_ref[...]


def _combine(x2d, route, g_final, y):
    t, d = x2d.shape
    tb = COMBINE_BM
    steps = t // tb
    return pl.pallas_call(
        _combine_kernel,
        out_shape=jax.ShapeDtypeStruct((t, d), F32),
        grid=(steps,),
        in_specs=[
            pl.BlockSpec((tb, d), lambda i: (i, 0)),
            pl.BlockSpec((tb, LANES), lambda i: (i, 0)),
            pl.BlockSpec((1, d), lambda i: (0, 0)),
            pl.BlockSpec((tb, d // 2), lambda i: (i, 0)),
            pl.BlockSpec((tb, d // 2), lambda i: (steps + i, 0)),
        ],
        out_specs=pl.BlockSpec((tb, d), lambda i: (i, 0)),
        compiler_params=_params(("parallel",), 40),
        name="combine",
    )(x2d, route, g_final, y, y)


def _layer(x, mem, g_mix, w_in, g_gm, w_spatial, b_spatial, w_branch_sb, w_branch_gm, w_out,
           g_cross, g_mem, w_xq, w_xkv, w_xo, g_ffn, w_rg, b_rg, w_re, b_re,
           w_e_gate, w_e_up, w_e_down, g_final):
    batch, seq, d = x.shape
    n_mem = mem.shape[1]
    t = batch * seq
    row = lambda v: v.reshape(1, -1)
    x2d = x.reshape(t, d)

    qkv, u, vn, gates = _inproj(x2d, row(g_mix), w_in.astype(BF16), row(g_gm))
    o_sb = _sb_attn(qkv, batch, seq)
    b_bcast = jnp.broadcast_to(b_spatial[:, :, None], b_spatial.shape + (GM_CHUNK,))
    merged = _merge(o_sb, u, vn, gates, w_spatial, b_bcast,
                    w_branch_sb.astype(BF16), w_branch_gm.astype(BF16))
    kv = _memkv(mem.reshape(batch * n_mem, d), row(g_mem), w_xkv.astype(BF16))
    pad = LANES - N_GROUPS - N_EXPERTS
    w_r = jnp.concatenate([w_rg, w_re, jnp.zeros((d, pad), F32)], axis=1)
    b_r = jnp.concatenate([b_rg, b_re, jnp.zeros((pad,), F32)]).reshape(1, LANES)
    w_r_hi = w_r.astype(BF16)
    w_r_lo = (w_r - w_r_hi.astype(F32)).astype(BF16)
    x2, route = _cross(x2d, merged, w_out.astype(BF16), row(g_cross), w_xq.astype(BF16), kv,
                       w_xo.astype(BF16), row(g_ffn), jnp.concatenate([w_r_hi, w_r_lo], axis=1),
                       b_r, seq, n_mem)

    n = t * TOP_K
    e_flat = route[:, :TOP_K].astype(I32).reshape(n)
    rank2d, cnt = _rank(e_flat.reshape(n // LANES, LANES))
    counts = cnt[:, 0].astype(I32)
    padded = ((counts + MOE_BM - 1) // MOE_BM) * MOE_BM
    pends = jnp.cumsum(padded)
    pstarts = pends - padded
    dest = pstarts[e_flat] + rank2d.reshape(n)
    n_blocks = n // MOE_BM + N_EXPERTS
    blk_start = jnp.arange(n_blocks, dtype=I32) * MOE_BM
    blk_e = jnp.minimum(jnp.sum(pends[None, :] <= blk_start[:, None], axis=1),
                        N_EXPERTS - 1).astype(I32)
    n_used = (pends[-1:] // MOE_BM).astype(I32)
    inv = jnp.full((n_blocks * MOE_BM,), -1, I32).at[dest].set(
        jnp.arange(n, dtype=I32), unique_indices=True, mode="promise_in_bounds")
    inv = inv.reshape(n_blocks, MOE_BM)
    tok = jnp.maximum(inv, 0) // TOP_K
    src3 = tok.reshape(n_blocks, 1, MOE_BM)
    spare = n + (jnp.arange(n_blocks, dtype=I32)[:, None] % 2) * MOE_BM + jnp.arange(MOE_BM, dtype=I32)
    dst = jnp.where(inv >= 0, (inv % TOP_K) * t + tok, spare)
    first = (n + MOE_BM + jnp.arange(MOE_BM, dtype=I32))[None]
    dst3 = jnp.concatenate([first, dst], axis=0).reshape(n_blocks + 1, 1, MOE_BM)

    y = _moe(x2, row(g_ffn), w_e_gate.astype(BF16), w_e_up.astype(BF16), w_e_down.astype(BF16),
             blk_e, n_used, src3, dst3, n_blocks, n + 2 * MOE_BM)
    out = _combine(x2, route, row(g_final), y)
    return out.reshape(batch, seq, d)


def kernel(x, mem, g_mix, w_in, g_gm, w_spatial, b_spatial, w_branch_sb, w_branch_gm, w_out,
           g_cross, g_mem, w_xq, w_xkv, w_xo, g_ffn, w_rg, b_rg, w_re, b_re,
           w_e_gate, w_e_up, w_e_down, g_final):
    assert w_in.shape[0] == 1, "single layer"
    return _layer(x, mem, g_mix[0], w_in[0], g_gm[0], w_spatial[0], b_spatial[0],
                  w_branch_sb[0], w_branch_gm[0], w_out[0], g_cross[0], g_mem[0], w_xq[0],
                  w_xkv[0], w_xo[0], g_ffn[0], w_rg[0], b_rg[0], w_re[0], b_re[0],
                  w_e_gate[0], w_e_up[0], w_e_down[0], g_final)
```
